```python
import math
import jax, jax.numpy as jnp
from jax import lax
import numpy as np

D_MODEL = 1024
BATCH = 4
SEQ = 4096
DEPTH = 4

D_MIX = D_MODEL
LRU_WIDTH = D_MIX // 2
LRU_BLOCKS = 8
LRU_BLOCK_W = LRU_WIDTH // LRU_BLOCKS
LRU_C = 8.0
CONV_WIDTH = 4
CONV_PAD = (CONV_WIDTH // 2, CONV_WIDTH - 1 - CONV_WIDTH // 2)
N_HEADS = 8
N_KV_HEADS = 2
KV_GROUP = N_HEADS // N_KV_HEADS
HEAD_DIM = (D_MIX - LRU_WIDTH) // N_HEADS
ATT_WIDTH = N_HEADS * HEAD_DIM
KV_WIDTH = N_KV_HEADS * HEAD_DIM
WINDOW = 128
BLOCK = 128
N_BUCKETS = 32
MAX_DISTANCE = 128
D_FF = ((8 * D_MODEL // 3 + 255) // 256) * 256
FFN_RES = 0.5
EPS = 1e-6
NEG_INF = -1e30
D_IN = 2 * LRU_WIDTH + ATT_WIDTH + 2 * KV_WIDTH
SPLITS = (LRU_WIDTH, 2 * LRU_WIDTH, 2 * LRU_WIDTH + ATT_WIDTH, 2 * LRU_WIDTH + ATT_WIDTH + KV_WIDTH)

kernel_name = 'hymba_style_rglru_swa_macaron_encoder'


def rms_norm(x, g):
    x32 = x.astype(jnp.float32)
    y = x32 * lax.rsqrt(jnp.mean(x32 * x32, axis=-1, keepdims=True) + EPS)
    return (y * g.astype(jnp.float32)).astype(x.dtype)


def swiglu(x, w_gate, w_up, w_down):
    return (jax.nn.silu(x @ w_gate) * (x @ w_up)) @ w_down


def t5_buckets(rel):
    half = N_BUCKETS // 2
    max_exact = half // 2
    ret = (rel > 0).astype(jnp.int32) * half
    n = jnp.abs(rel)
    n_f = jnp.maximum(n, 1).astype(jnp.float32)
    large = max_exact + (jnp.log(n_f / max_exact) / math.log(MAX_DISTANCE / max_exact) * (half - max_exact)).astype(jnp.int32)
    large = jnp.minimum(large, half - 1)
    return ret + jnp.where(n < max_exact, n, large)


def band_layout(seq):
    nb = seq // BLOCK
    n_idx = jnp.arange(nb)[:, None, None]
    t = jnp.arange(BLOCK)[None, :, None]
    j = jnp.arange(3 * BLOCK)[None, None, :]
    rel = j - BLOCK - t
    key_pos = (n_idx - 1) * BLOCK + j
    mask = (jnp.abs(rel) <= WINDOW) & (key_pos >= 0) & (key_pos < seq)
    return t5_buckets(rel[0]), mask


def band_windows(t, nb):
    b = t.shape[0]
    tp = jnp.pad(t, ((0, 0), (BLOCK, BLOCK), (0, 0), (0, 0)))
    tb = tp.reshape(b, nb + 2, BLOCK, N_KV_HEADS, HEAD_DIM)
    return jnp.concatenate([tb[:, :-2], tb[:, 1:-1], tb[:, 2:]], axis=2)


def windowed_gqa(q, k, v, sink, rel_bias):
    b, s = q.shape[0], q.shape[1]
    nb = s // BLOCK
    qb = q.reshape(b, nb, BLOCK, N_KV_HEADS, KV_GROUP, HEAD_DIM) * (HEAD_DIM ** -0.5)
    kw = band_windows(k.reshape(b, s, N_KV_HEADS, HEAD_DIM), nb)
    vw = band_windows(v.reshape(b, s, N_KV_HEADS, HEAD_DIM), nb)
    buckets, mask = band_layout(s)
    bias = rel_bias[buckets].astype(jnp.float32)
    bias = jnp.transpose(bias, (2, 0, 1)).reshape(N_KV_HEADS, KV_GROUP, BLOCK, 3 * BLOCK)
    logits = jnp.einsum('bnqkgd,bnjkd->bnkgqj', qb, kw).astype(jnp.float32) + bias
    logits = jnp.where(mask[None, :, None, None], logits, NEG_INF)
    sink32 = sink.astype(jnp.float32).reshape(1, 1, N_KV_HEADS, KV_GROUP, 1, 1)
    m = jnp.maximum(jnp.max(logits, axis=-1, keepdims=True), sink32)
    p = jnp.exp(logits - m)
    p = p / (jnp.sum(p, axis=-1, keepdims=True) + jnp.exp(sink32 - m))
    o = jnp.einsum('bnkgqj,bnjkd->bnqkgd', p.astype(vw.dtype), vw)
    return o.reshape(b, s, ATT_WIDTH)


def _linear_combine(left, right):
    a_l, b_l = left
    a_r, b_r = right
    return a_l * a_r, a_r * b_l + b_r


def rg_lru_direction(xc, w_a, b_a, w_x, b_x, lam, reverse):
    b, s = xc.shape[0], xc.shape[1]
    xb = xc.reshape(b, s, LRU_BLOCKS, LRU_BLOCK_W)
    r = jax.nn.sigmoid(jnp.einsum('bsnc,ncd->bsnd', xb, w_a).reshape(b, s, LRU_WIDTH).astype(jnp.float32) + b_a.astype(jnp.float32))
    i = jax.nn.sigmoid(jnp.einsum('bsnc,ncd->bsnd', xb, w_x).reshape(b, s, LRU_WIDTH).astype(jnp.float32) + b_x.astype(jnp.float32))
    log_a = -LRU_C * jax.nn.softplus(-lam.astype(jnp.float32)) * r
    a = jnp.exp(log_a)
    u = jnp.sqrt(-jnp.expm1(2.0 * log_a)) * (i * xc.astype(jnp.float32))
    _, h = lax.associative_scan(_linear_combine, (a, u), axis=1, reverse=reverse)
    return h


def recurrent_group(xr, gate, conv_w, conv_b, w_a, b_a, w_x, b_x, lam):
    xc = lax.conv_general_dilated(xr, conv_w[:, None, :], window_strides=(1,), padding=[CONV_PAD],
                                  dimension_numbers=('NWC', 'WIO', 'NWC'), feature_group_count=LRU_WIDTH) + conv_b
    h = (rg_lru_direction(xc, w_a[0], b_a[0], w_x[0], b_x[0], lam[0], False)
         + rg_lru_direction(xc, w_a[1], b_a[1], w_x[1], b_x[1], lam[1], True))
    return (jax.nn.gelu(gate.astype(jnp.float32)) * h).astype(xr.dtype)


def setup_inputs(seed: int = 0) -> dict:
    key = jax.random.key(seed)
    ks = jax.random.split(key, 32)
    f32 = jnp.float32

    def nrm(k, shape, fan_in):
        return jax.random.normal(k, shape, f32) * (fan_in ** -0.5)

    def gain(k, shape):
        return 1.0 + 0.02 * jax.random.normal(k, shape, f32)

    def small(k, shape, scale=0.01):
        return scale * jax.random.normal(k, shape, f32)

    u = jax.random.uniform(ks[14], (DEPTH, 2, LRU_WIDTH), f32, minval=0.9, maxval=0.999)
    a0 = u ** (1.0 / LRU_C)
    lru_lambda = jnp.log(a0) - jnp.log1p(-a0)
    return {
        'x': jax.random.normal(ks[0], (BATCH, SEQ, D_MODEL), f32),
        'ffn1_norm': gain(ks[1], (DEPTH, D_MODEL)),
        'ffn1_w_gate': nrm(ks[2], (DEPTH, D_MODEL, D_FF), D_MODEL),
        'ffn1_w_up': nrm(ks[3], (DEPTH, D_MODEL, D_FF), D_MODEL),
        'ffn1_w_down': nrm(ks[4], (DEPTH, D_FF, D_MODEL), D_FF),
        'mix_norm': gain(ks[5], (DEPTH, D_MODEL)),
        'w_in': nrm(ks[6], (DEPTH, D_MODEL, D_IN), D_MODEL),
        'conv_w': nrm(ks[7], (DEPTH, CONV_WIDTH, LRU_WIDTH), CONV_WIDTH),
        'conv_b': small(ks[8], (DEPTH, LRU_WIDTH)),
        'lru_w_a': nrm(ks[9], (DEPTH, 2, LRU_BLOCKS, LRU_BLOCK_W, LRU_BLOCK_W), LRU_BLOCK_W),
        'lru_b_a': small(ks[10], (DEPTH, 2, LRU_WIDTH), 0.1),
        'lru_w_x': nrm(ks[11], (DEPTH, 2, LRU_BLOCKS, LRU_BLOCK_W, LRU_BLOCK_W), LRU_BLOCK_W),
        'lru_b_x': small(ks[12], (DEPTH, 2, LRU_WIDTH), 0.1),
        'lru_lambda': lru_lambda,
        'attn_sink': 0.5 * jax.random.normal(ks[15], (DEPTH, N_HEADS), f32),
        'rel_bias': 0.2 * jax.random.normal(ks[16], (N_BUCKETS, N_HEADS), f32),
        'lru_out_norm': gain(ks[17], (DEPTH, LRU_WIDTH)),
        'attn_out_norm': gain(ks[18], (DEPTH, ATT_WIDTH)),
        'w_out': nrm(ks[19], (DEPTH, D_MIX, D_MODEL), D_MIX),
        'ffn2_norm': gain(ks[20], (DEPTH, D_MODEL)),
        'ffn2_w_gate': nrm(ks[21], (DEPTH, D_MODEL, D_FF), D_MODEL),
        'ffn2_w_up': nrm(ks[22], (DEPTH, D_MODEL, D_FF), D_MODEL),
        'ffn2_w_down': nrm(ks[23], (DEPTH, D_FF, D_MODEL), D_FF),
        'final_norm': gain(ks[24], (D_MODEL,)),
    }


def reference(x, ffn1_norm, ffn1_w_gate, ffn1_w_up, ffn1_w_down, mix_norm, w_in, conv_w, conv_b,
              lru_w_a, lru_b_a, lru_w_x, lru_b_x, lru_lambda, attn_sink, rel_bias,
              lru_out_norm, attn_out_norm, w_out, ffn2_norm, ffn2_w_gate, ffn2_w_up, ffn2_w_down,
              final_norm):
    for l in range(DEPTH):
        x = x + FFN_RES * swiglu(rms_norm(x, ffn1_norm[l]), ffn1_w_gate[l], ffn1_w_up[l], ffn1_w_down[l])
        h = rms_norm(x, mix_norm[l])
        proj = h @ w_in[l]
        xr, gate, q, k, v = jnp.split(proj, SPLITS, axis=-1)
        y_rec = recurrent_group(xr, gate, conv_w[l], conv_b[l], lru_w_a[l], lru_b_a[l],
                                lru_w_x[l], lru_b_x[l], lru_lambda[l])
        y_att = windowed_gqa(q, k, v, attn_sink[l], rel_bias)
        y = jnp.concatenate([rms_norm(y_rec, lru_out_norm[l]), rms_norm(y_att, attn_out_norm[l])], axis=-1)
        x = x + y @ w_out[l]
        x = x + FFN_RES * swiglu(rms_norm(x, ffn2_norm[l]), ffn2_w_gate[l], ffn2_w_up[l], ffn2_w_down[l])
    return rms_norm(x, final_norm)
```

```python
import functools
import math

import jax
import jax.numpy as jnp
from jax import lax
from jax.experimental import pallas as pl
from jax.experimental.pallas import tpu as pltpu

F32 = jnp.float32
BF16 = jnp.bfloat16

D_MODEL = 1024
LRU_WIDTH = 512
LRU_BLOCK_W = 64
LRU_C = 8.0
CONV_WIDTH = 4
N_HEADS = 8
N_KV_HEADS = 2
KV_GROUP = N_HEADS // N_KV_HEADS
HEAD_DIM = 64
ATT_WIDTH = N_HEADS * HEAD_DIM
KV_WIDTH = N_KV_HEADS * HEAD_DIM
WINDOW = 128
BLOCK = 128
N_BUCKETS = 32
MAX_DISTANCE = 128
D_FF = 2816
FFN_RES = 0.5
EPS = 1e-6
NEG_INF = -1e30
D_IN = 2 * LRU_WIDTH + ATT_WIDTH + 2 * KV_WIDTH

LANES = 128
SUBLANES = 8
FF_CHUNK = 256
N_FF_CHUNKS = D_FF // FF_CHUNK
TOKEN_TILE = 512
VMEM_LIMIT = 56 * 1024 * 1024

SEG_PITCH = 516
REC_CHUNK = 512
PAD_ROWS = SUBLANES


def _rms(x, g):
    return x * lax.rsqrt(jnp.mean(x * x, axis=-1, keepdims=True) + EPS) * g


def _sigmoid(x):
    return 1.0 / (1.0 + jnp.exp(-x))


def _swiglu_into(h_ref, wgu_ref, wd_ref, acc_ref):
    acc_ref[...] = jnp.zeros_like(acc_ref)

    def chunk(c, carry):
        gu = jnp.dot(h_ref[...], wgu_ref[c], preferred_element_type=F32)
        g = gu[:, :FF_CHUNK]
        u = gu[:, FF_CHUNK:]
        act = (g * _sigmoid(g) * u).astype(BF16)
        acc_ref[...] += jnp.dot(act, wd_ref[c], preferred_element_type=F32)
        return carry

    lax.fori_loop(0, N_FF_CHUNKS, chunk, 0)


def _ffn1_inproj_kernel(x_ref, n1_ref, wgu_ref, wd_ref, nm_ref, win_ref,
                        xo_ref, xr_ref, gate_ref, q_ref, k_ref, v_ref,
                        h_ref, acc_ref):
    x = x_ref[...]
    h_ref[...] = _rms(x, n1_ref[...]).astype(BF16)
    _swiglu_into(h_ref, wgu_ref, wd_ref, acc_ref)
    x1 = x + FFN_RES * acc_ref[...]
    xo_ref[...] = x1
    hm = _rms(x1, nm_ref[...]).astype(BF16)
    proj = jnp.dot(hm, win_ref[...], preferred_element_type=F32)
    xr_ref[...] = proj[:, :LRU_WIDTH]
    gate_ref[...] = proj[:, LRU_WIDTH:2 * LRU_WIDTH]
    o = 2 * LRU_WIDTH
    q_ref[...] = (proj[:, o:o + ATT_WIDTH] * (HEAD_DIM ** -0.5)).astype(BF16)
    o += ATT_WIDTH
    k_ref[...] = proj[:, o:o + KV_WIDTH].astype(BF16)
    o += KV_WIDTH
    v_ref[...] = proj[:, o:o + KV_WIDTH].astype(BF16)


def _outproj_ffn2_kernel(x_ref, yr_ref, ya_ref, nr_ref, wo_ref, n2_ref, wgu_ref, wd_ref, nf_ref,
                         xo_ref, h_ref, acc_ref, *, final_norm):
    yr = _rms(yr_ref[...], nr_ref[...]).astype(BF16)
    x1 = (x_ref[...]
          + jnp.dot(yr, wo_ref[:LRU_WIDTH, :], preferred_element_type=F32)
          + jnp.dot(ya_ref[...], wo_ref[LRU_WIDTH:, :], preferred_element_type=F32))
    h_ref[...] = _rms(x1, n2_ref[...]).astype(BF16)
    _swiglu_into(h_ref, wgu_ref, wd_ref, acc_ref)
    x2 = x1 + FFN_RES * acc_ref[...]
    if final_norm:
        x2 = _rms(x2, nf_ref[...])
    xo_ref[...] = x2


def _resident(shape):
    nd = len(shape)
    return pl.BlockSpec(shape, lambda i: (0,) * nd, pipeline_mode=pl.Buffered(1))


def _rows(width, tm=TOKEN_TILE):
    return pl.BlockSpec((tm, width), lambda i: (i, 0))


def _ffn1_inproj(x, n1, wgu, wd, nm, win):
    t = x.shape[0]
    tm = TOKEN_TILE
    out_shape = (
        jax.ShapeDtypeStruct((t, D_MODEL), F32),
        jax.ShapeDtypeStruct((t, LRU_WIDTH), F32),
        jax.ShapeDtypeStruct((t, LRU_WIDTH), F32),
        jax.ShapeDtypeStruct((t, ATT_WIDTH), BF16),
        jax.ShapeDtypeStruct((t, KV_WIDTH), BF16),
        jax.ShapeDtypeStruct((t, KV_WIDTH), BF16),
    )
    return pl.pallas_call(
        _ffn1_inproj_kernel,
        grid=(t // tm,),
        in_specs=[_rows(D_MODEL), _resident(n1.shape), _resident(wgu.shape), _resident(wd.shape),
                  _resident(nm.shape), _resident(win.shape)],
        out_specs=(_rows(D_MODEL), _rows(LRU_WIDTH), _rows(LRU_WIDTH), _rows(ATT_WIDTH),
                   _rows(KV_WIDTH), _rows(KV_WIDTH)),
        out_shape=out_shape,
        scratch_shapes=[pltpu.VMEM((tm, D_MODEL), BF16), pltpu.VMEM((tm, D_MODEL), F32)],
        compiler_params=pltpu.CompilerParams(dimension_semantics=("arbitrary",),
                                             vmem_limit_bytes=VMEM_LIMIT),
        name="ffn1_inproj",
    )(x, n1, wgu, wd, nm, win)


def _outproj_ffn2(x, yr, ya, nr, wo, n2, wgu, wd, nf, final_norm):
    t = x.shape[0]
    tm = TOKEN_TILE
    return pl.pallas_call(
        functools.partial(_outproj_ffn2_kernel, final_norm=final_norm),
        grid=(t // tm,),
        in_specs=[_rows(D_MODEL), _rows(LRU_WIDTH), _rows(ATT_WIDTH), _resident(nr.shape),
                  _resident(wo.shape), _resident(n2.shape), _resident(wgu.shape),
                  _resident(wd.shape), _resident(nf.shape)],
        out_specs=_rows(D_MODEL),
        out_shape=jax.ShapeDtypeStruct((t, D_MODEL), F32),
        scratch_shapes=[pltpu.VMEM((tm, D_MODEL), BF16), pltpu.VMEM((tm, D_MODEL), F32)],
        compiler_params=pltpu.CompilerParams(dimension_semantics=("arbitrary",),
                                             vmem_limit_bytes=VMEM_LIMIT),
        name="outproj_ffn2",
    )(x, yr, ya, nr, wo, n2, wgu, wd, nf)


def _rec_kernel(xr_ref, gate_ref, cw_ref, cb_ref, wg_ref, bg_ref, lam_ref, o_ref,
                xpad, a_f, u_f, a_b, u_b, *, seq):
    n_chunks = seq // REC_CHUNK
    n_steps = SEG_PITCH
    pad_rows = SUBLANES * SEG_PITCH - seq

    zeros_pad = jnp.zeros((PAD_ROWS, LANES), F32)
    xpad[0:PAD_ROWS, :] = zeros_pad
    xpad[PAD_ROWS + seq:2 * PAD_ROWS + seq, :] = zeros_pad
    xpad[PAD_ROWS:PAD_ROWS + seq, :] = xr_ref[...]

    for a_ref, u_ref in ((a_f, u_f), (a_b, u_b)):
        a_ref[seq:seq + pad_rows, :] = jnp.ones((pad_rows, LANES), F32)
        u_ref[seq:seq + pad_rows, :] = jnp.zeros((pad_rows, LANES), F32)

    lam = lam_ref[...]
    neg_softplus = -(jnp.maximum(-lam, 0.0) + jnp.log1p(jnp.exp(-jnp.abs(lam))))
    log_a_scale = LRU_C * neg_softplus

    cw = cw_ref[...]
    cb = cb_ref[...]
    for c in range(n_chunks):
        base = PAD_ROWS + c * REC_CHUNK
        xc = cb
        for tap in range(CONV_WIDTH):
            lo = base + tap - CONV_WIDTH // 2
            xc = xc + cw[tap:tap + 1, :] * xpad[lo:lo + REC_CHUNK, :]
        gates = jnp.dot(xc.astype(BF16), wg_ref[...], preferred_element_type=F32) + bg_ref[...]
        rows = slice(c * REC_CHUNK, (c + 1) * REC_CHUNK)
        for d, (a_ref, u_ref) in enumerate(((a_f, u_f), (a_b, u_b))):
            r = _sigmoid(gates[:, (2 * d) * LANES:(2 * d + 1) * LANES])
            i = _sigmoid(gates[:, (2 * d + 1) * LANES:(2 * d + 2) * LANES])
            a = jnp.exp(log_a_scale[d:d + 1, :] * r)
            a_ref[rows, :] = a
            u_ref[rows, :] = jnp.sqrt(1.0 - a * a) * (i * xc)

    def seg_rows(j):
        return pl.ds(j, SUBLANES, stride=SEG_PITCH)

    def step(j, carry):
        hf, pf, hb, pb = carry
        jb = n_steps - 1 - j
        af = a_f[seg_rows(j), :]
        hf = af * hf + u_f[seg_rows(j), :]
        pf = pf * af
        u_f[seg_rows(j), :] = hf
        a_f[seg_rows(j), :] = pf
        ab = a_b[seg_rows(jb), :]
        hb = ab * hb + u_b[seg_rows(jb), :]
        pb = pb * ab
        u_b[seg_rows(jb), :] = hb
        a_b[seg_rows(jb), :] = pb
        return hf, pf, hb, pb

    zero = jnp.zeros((SUBLANES, LANES), F32)
    one = jnp.ones((SUBLANES, LANES), F32)
    hf_end, pf_end, hb_end, pb_end = lax.fori_loop(0, n_steps, step, (zero, one, zero, one), unroll=4)

    seg = lax.broadcasted_iota(jnp.int32, (SUBLANES, LANES), 0)
    pf_in = pltpu.roll(pf_end, 1, 0)
    hf_in = pltpu.roll(hf_end, 1, 0)
    pb_in = pltpu.roll(pb_end, SUBLANES - 1, 0)
    hb_in = pltpu.roll(hb_end, SUBLANES - 1, 0)
    cf = zero
    cbk = zero
    for _ in range(SUBLANES - 1):
        cf = jnp.where(seg == 0, 0.0, pf_in * pltpu.roll(cf, 1, 0) + hf_in)
        cbk = jnp.where(seg == SUBLANES - 1, 0.0, pb_in * pltpu.roll(cbk, SUBLANES - 1, 0) + hb_in)

    row = lax.broadcasted_iota(jnp.int32, (REC_CHUNK, LANES), 0)
    for c in range(n_chunks):
        rows = slice(c * REC_CHUNK, (c + 1) * REC_CHUNK)
        s_lo = (c * REC_CHUNK) // SEG_PITCH
        s_hi = ((c + 1) * REC_CHUNK - 1) // SEG_PITCH
        split = (s_lo + 1) * SEG_PITCH - c * REC_CHUNK
        if s_hi == s_lo:
            cf_rows = cf[s_lo:s_lo + 1, :]
            cb_rows = cbk[s_lo:s_lo + 1, :]
        else:
            cf_rows = jnp.where(row < split, cf[s_lo:s_lo + 1, :], cf[s_hi:s_hi + 1, :])
            cb_rows = jnp.where(row < split, cbk[s_lo:s_lo + 1, :], cbk[s_hi:s_hi + 1, :])
        h = (u_f[rows, :] + a_f[rows, :] * cf_rows) + (u_b[rows, :] + a_b[rows, :] * cb_rows)
        o_ref[rows, :] = jax.nn.gelu(gate_ref[rows, :]) * h


def _rec(xr, gate, cw, cb, wg, bg, lam, batch, seq):
    n_slabs = LRU_WIDTH // LANES
    n_seq_blocks = 1
    assert SUBLANES * SEG_PITCH >= seq and seq % REC_CHUNK == 0 and (SUBLANES - 1) * SEG_PITCH < seq
    slab = pl.BlockSpec((seq, LANES), lambda b, s: (b * n_seq_blocks, s))
    scan_rows = SUBLANES * SEG_PITCH
    return pl.pallas_call(
        functools.partial(_rec_kernel, seq=seq),
        grid=(batch, n_slabs),
        in_specs=[slab, slab,
                  pl.BlockSpec((CONV_WIDTH, LANES), lambda b, s: (0, s)),
                  pl.BlockSpec((1, LANES), lambda b, s: (0, s)),
                  pl.BlockSpec((None, LANES, 4 * LANES), lambda b, s: (s, 0, 0)),
                  pl.BlockSpec((None, 1, 4 * LANES), lambda b, s: (s, 0, 0)),
                  pl.BlockSpec((2, LANES), lambda b, s: (0, s))],
        out_specs=slab,
        out_shape=jax.ShapeDtypeStruct((batch * seq, LRU_WIDTH), F32),
        scratch_shapes=[pltpu.VMEM((seq + 2 * PAD_ROWS, LANES), F32)]
                       + [pltpu.VMEM((scan_rows, LANES), F32)] * 4,
        compiler_params=pltpu.CompilerParams(dimension_semantics=("arbitrary", "arbitrary"),
                                             vmem_limit_bytes=VMEM_LIMIT),
        name="rg_lru",
    )(xr, gate, cw, cb, wg, bg, lam)


def _bias_kernel(buckets_ref, table_ref, o_ref):
    buckets = buckets_ref[...]
    for h in range(N_HEADS):
        acc = jnp.zeros(buckets.shape, F32)
        for b in range(N_BUCKETS):
            acc = jnp.where(buckets == b, table_ref[b, h], acc)
        o_ref[h] = acc


def _rel_bias_table(buckets, rel_bias):
    return pl.pallas_call(
        _bias_kernel,
        in_specs=[pl.BlockSpec(memory_space=pltpu.VMEM), pl.BlockSpec(memory_space=pltpu.SMEM)],
        out_specs=pl.BlockSpec(memory_space=pltpu.VMEM),
        out_shape=jax.ShapeDtypeStruct((N_HEADS,) + buckets.shape, F32),
        name="rel_bias",
    )(buckets, rel_bias)


def _attn_kernel(q_ref, k_ref, v_ref, bias_ref, sink_ref, gn_ref, o_ref, kpad, vpad, *, seq):
    nb = seq // BLOCK
    zeros_blk = jnp.zeros((BLOCK, KV_WIDTH), BF16)
    for pad_ref, src in ((kpad, k_ref), (vpad, v_ref)):
        pad_ref[0:BLOCK, :] = zeros_blk
        pad_ref[BLOCK + seq:2 * BLOCK + seq, :] = zeros_blk
        pad_ref[BLOCK:BLOCK + seq, :] = src[...]

    t_idx = lax.broadcasted_iota(jnp.int32, (BLOCK, 3 * BLOCK), 0)
    j_idx = lax.broadcasted_iota(jnp.int32, (BLOCK, 3 * BLOCK), 1)
    band = jnp.abs(j_idx - BLOCK - t_idx) <= WINDOW
    gn = gn_ref[...]

    def block(n, carry):
        start = pl.multiple_of(n * BLOCK, BLOCK)
        key_pos = j_idx + (n - 1) * BLOCK
        valid = band & (key_pos >= 0) & (key_pos < seq)
        qblk = q_ref[pl.ds(start, BLOCK), :]
        kwin = kpad[pl.ds(start, 3 * BLOCK), :]
        vwin = vpad[pl.ds(start, 3 * BLOCK), :]
        outs = []
        for g in range(N_KV_HEADS):
            kg = kwin[:, g * HEAD_DIM:(g + 1) * HEAD_DIM]
            vg = vwin[:, g * HEAD_DIM:(g + 1) * HEAD_DIM]
            for hh in range(KV_GROUP):
                h = g * KV_GROUP + hh
                qh = qblk[:, h * HEAD_DIM:(h + 1) * HEAD_DIM]
                s = lax.dot_general(qh, kg, (((1,), (1,)), ((), ())), preferred_element_type=F32)
                s = jnp.where(valid, s + bias_ref[h], NEG_INF)
                sink = sink_ref[h]
                m = jnp.maximum(jnp.max(s, axis=-1, keepdims=True), sink)
                p = jnp.exp(s - m)
                den = jnp.sum(p, axis=-1, keepdims=True) + jnp.exp(sink - m)
                o = jnp.dot(p.astype(BF16), vg, preferred_element_type=F32)
                outs.append(o / den)
        y = jnp.concatenate(outs, axis=-1)
        o_ref[pl.ds(start, BLOCK), :] = _rms(y, gn).astype(BF16)
        return carry

    lax.fori_loop(0, nb, block, 0)


def _attn(q, k, v, bias, sink, gn, batch, seq):
    def seq_rows(width):
        return pl.BlockSpec((seq, width), lambda b: (b, 0))

    return pl.pallas_call(
        functools.partial(_attn_kernel, seq=seq),
        grid=(batch,),
        in_specs=[seq_rows(ATT_WIDTH), seq_rows(KV_WIDTH), seq_rows(KV_WIDTH),
                  pl.BlockSpec(bias.shape, lambda b: (0, 0, 0), pipeline_mode=pl.Buffered(1)),
                  pl.BlockSpec(memory_space=pltpu.SMEM),
                  pl.BlockSpec(gn.shape, lambda b: (0, 0), pipeline_mode=pl.Buffered(1))],
        out_specs=seq_rows(ATT_WIDTH),
        out_shape=jax.ShapeDtypeStruct((batch * seq, ATT_WIDTH), BF16),
        scratch_shapes=[pltpu.VMEM((seq + 2 * BLOCK, KV_WIDTH), BF16)] * 2,
        compiler_params=pltpu.CompilerParams(dimension_semantics=("arbitrary",),
                                             vmem_limit_bytes=VMEM_LIMIT),
        name="windowed_gqa",
    )(q, k, v, bias, sink, gn)


def _t5_bucket_index():
    t = jnp.arange(BLOCK)[:, None]
    j = jnp.arange(3 * BLOCK)[None, :]
    rel = j - BLOCK - t
    half = N_BUCKETS // 2
    max_exact = half // 2
    ret = (rel > 0).astype(jnp.int32) * half
    n = jnp.abs(rel)
    n_f = jnp.maximum(n, 1).astype(jnp.float32)
    large = max_exact + (jnp.log(n_f / max_exact) / math.log(MAX_DISTANCE / max_exact)
                         * (half - max_exact)).astype(jnp.int32)
    large = jnp.minimum(large, half - 1)
    return ret + jnp.where(n < max_exact, n, large)


def _ffn_weights(w_gate, w_up, w_down):
    wg = w_gate.astype(BF16).reshape(D_MODEL, N_FF_CHUNKS, FF_CHUNK)
    wu = w_up.astype(BF16).reshape(D_MODEL, N_FF_CHUNKS, FF_CHUNK)
    wgu = jnp.transpose(jnp.concatenate([wg, wu], axis=-1), (1, 0, 2))
    wd = w_down.astype(BF16).reshape(N_FF_CHUNKS, FF_CHUNK, D_MODEL)
    return wgu, wd


def _gate_weights(w_a, b_a, w_x, b_x):
    n_slabs = LRU_WIDTH // LANES
    per_slab = LANES // LRU_BLOCK_W

    def slab_blockdiag(w):
        w = w.reshape(n_slabs, per_slab, LRU_BLOCK_W, LRU_BLOCK_W)
        eye = jnp.eye(per_slab, dtype=w.dtype)
        full = jnp.einsum('spcd,pq->spcqd', w, eye)
        return full.reshape(n_slabs, LANES, LANES)

    mats = [slab_blockdiag(w_a[0]), slab_blockdiag(w_x[0]), slab_blockdiag(w_a[1]), slab_blockdiag(w_x[1])]
    wg = jnp.concatenate(mats, axis=-1).astype(BF16)
    biases = [b_a[0], b_x[0], b_a[1], b_x[1]]
    bg = jnp.concatenate([b.reshape(n_slabs, 1, LANES) for b in biases], axis=-1)
    return wg, bg


def kernel(x, ffn1_norm, ffn1_w_gate, ffn1_w_up, ffn1_w_down, mix_norm, w_in, conv_w, conv_b, lru_w_a, lru_b_a, lru_w_x, lru_b_x, lru_lambda, attn_sink, rel_bias, lru_out_norm, attn_out_norm, w_out, ffn2_norm, ffn2_w_gate, ffn2_w_up, ffn2_w_down, final_norm):
    batch, seq, d_model = x.shape
    depth = ffn1_norm.shape[0]
    assert d_model == D_MODEL and seq % BLOCK == 0 and (batch * seq) % TOKEN_TILE == 0
    xt = x.reshape(batch * seq, D_MODEL)
    bias = _rel_bias_table(_t5_bucket_index(), rel_bias)
    row = lambda p: p.reshape(1, -1)
    for l in range(depth):
        wgu1, wd1 = _ffn_weights(ffn1_w_gate[l], ffn1_w_up[l], ffn1_w_down[l])
        wgu2, wd2 = _ffn_weights(ffn2_w_gate[l], ffn2_w_up[l], ffn2_w_down[l])
        wg, bg = _gate_weights(lru_w_a[l], lru_b_a[l], lru_w_x[l], lru_b_x[l])
        xt, xr, gate, q, k, v = _ffn1_inproj(xt, row(ffn1_norm[l]), wgu1, wd1, row(mix_norm[l]),
                                             w_in[l].astype(BF16))
        y_rec = _rec(xr, gate, conv_w[l], row(conv_b[l]), wg, bg, lru_lambda[l], batch, seq)
        y_att = _attn(q, k, v, bias, attn_sink[l], row(attn_out_norm[l]), batch, seq)
        xt = _outproj_ffn2(xt, y_rec, y_att, row(lru_out_norm[l]), w_out[l].astype(BF16),
                           row(ffn2_norm[l]), wgu2, wd2, row(final_norm), final_norm=(l == depth - 1))
    return xt.reshape(batch, seq, D_MODEL)
```

```python
import functools
import math

import jax
import jax.numpy as jnp
from jax import lax
from jax.experimental import pallas as pl
from jax.experimental.pallas import tpu as pltpu

F32 = jnp.float32
BF16 = jnp.bfloat16

D_MODEL = 1024
LRU_WIDTH = 512
LRU_BLOCK_W = 64
LRU_C = 8.0
CONV_WIDTH = 4
N_HEADS = 8
N_KV_HEADS = 2
KV_GROUP = N_HEADS // N_KV_HEADS
HEAD_DIM = 64
ATT_WIDTH = N_HEADS * HEAD_DIM
KV_WIDTH = N_KV_HEADS * HEAD_DIM
WINDOW = 128
BLOCK = 128
N_BUCKETS = 32
MAX_DISTANCE = 128
D_FF = 2816
FFN_RES = 0.5
EPS = 1e-6
NEG_INF = -1e30
D_IN = 2 * LRU_WIDTH + ATT_WIDTH + 2 * KV_WIDTH
LOG2E = math.log2(math.e)

LANES = 128
SUBLANES = 8
FF_CHUNK = 256
N_FF_CHUNKS = D_FF // FF_CHUNK
TOKEN_TILE = 512
VMEM_LIMIT = 56 * 1024 * 1024

SCAN_SEGS = 32
SCAN_VREGS = SCAN_SEGS // SUBLANES
SEG_PITCH = 129
REC_CHUNK = 512
PAD_ROWS = SUBLANES


def _rms(x, g):
    return x * lax.rsqrt(jnp.mean(x * x, axis=-1, keepdims=True) + EPS) * g


def _sigmoid(x):
    return 0.5 * jnp.tanh(0.5 * x) + 0.5


def _swiglu_into(h_ref, wg_ref, wu_ref, wd_ref, acc_ref):
    for c in range(N_FF_CHUNKS):
        cols = slice(c * FF_CHUNK, (c + 1) * FF_CHUNK)
        g = jnp.dot(h_ref[...], wg_ref[:, cols], preferred_element_type=F32)
        u = jnp.dot(h_ref[...], wu_ref[:, cols], preferred_element_type=F32)
        act = (g * _sigmoid(g) * u).astype(BF16)
        part = jnp.dot(act, wd_ref[cols, :], preferred_element_type=F32)
        if c == 0:
            acc_ref[...] = part
        else:
            acc_ref[...] += part


def _ffn1_inproj_kernel(x_ref, n1_ref, wg_ref, wu_ref, wd_ref, nm_ref, win_ref,
                        xo_ref, xr_ref, gate_ref, q_ref, k_ref, v_ref,
                        h_ref, acc_ref):
    x = x_ref[...]
    h_ref[...] = _rms(x, n1_ref[...]).astype(BF16)
    _swiglu_into(h_ref, wg_ref, wu_ref, wd_ref, acc_ref)
    x1 = x + FFN_RES * acc_ref[...]
    xo_ref[...] = x1
    hm = _rms(x1, nm_ref[...]).astype(BF16)
    proj = jnp.dot(hm, win_ref[...], preferred_element_type=F32)
    xr_ref[...] = proj[:, :LRU_WIDTH]
    gate_ref[...] = jax.nn.gelu(proj[:, LRU_WIDTH:2 * LRU_WIDTH])
    o = 2 * LRU_WIDTH
    q_ref[...] = (proj[:, o:o + ATT_WIDTH] * (HEAD_DIM ** -0.5 * LOG2E)).astype(BF16)
    o += ATT_WIDTH
    k_ref[...] = proj[:, o:o + KV_WIDTH].astype(BF16)
    o += KV_WIDTH
    v_ref[...] = proj[:, o:o + KV_WIDTH].astype(BF16)


def _outproj_ffn2_kernel(x_ref, yr_ref, ya_ref, nr_ref, wo_ref, n2_ref, wg_ref, wu_ref, wd_ref, nf_ref,
                         xo_ref, h_ref, acc_ref, *, final_norm):
    yr = _rms(yr_ref[...], nr_ref[...]).astype(BF16)
    x1 = (x_ref[...]
          + jnp.dot(yr, wo_ref[:LRU_WIDTH, :], preferred_element_type=F32)
          + jnp.dot(ya_ref[...], wo_ref[LRU_WIDTH:, :], preferred_element_type=F32))
    h_ref[...] = _rms(x1, n2_ref[...]).astype(BF16)
    _swiglu_into(h_ref, wg_ref, wu_ref, wd_ref, acc_ref)
    x2 = x1 + FFN_RES * acc_ref[...]
    if final_norm:
        x2 = _rms(x2, nf_ref[...])
    xo_ref[...] = x2


def _layer(stacked, l):
    assert stacked.ndim == 3
    return pl.BlockSpec((None,) + stacked.shape[1:], lambda i: (l, 0, 0), pipeline_mode=pl.Buffered(1))


def _rows(width, tm=TOKEN_TILE):
    return pl.BlockSpec((tm, width), lambda i: (i, 0))


def _ffn1_inproj(l, x, n1, wg, wu, wd, nm, win):
    t = x.shape[0]
    tm = TOKEN_TILE
    out_shape = (
        jax.ShapeDtypeStruct((t, D_MODEL), F32),
        jax.ShapeDtypeStruct((t, LRU_WIDTH), F32),
        jax.ShapeDtypeStruct((t, LRU_WIDTH), F32),
        jax.ShapeDtypeStruct((t, ATT_WIDTH), BF16),
        jax.ShapeDtypeStruct((t, KV_WIDTH), BF16),
        jax.ShapeDtypeStruct((t, KV_WIDTH), BF16),
    )
    params = (n1, wg, wu, wd, nm, win)
    return pl.pallas_call(
        _ffn1_inproj_kernel,
        grid=(t // tm,),
        in_specs=[_rows(D_MODEL)] + [_layer(p, l) for p in params],
        out_specs=(_rows(D_MODEL), _rows(LRU_WIDTH), _rows(LRU_WIDTH), _rows(ATT_WIDTH),
                   _rows(KV_WIDTH), _rows(KV_WIDTH)),
        out_shape=out_shape,
        scratch_shapes=[pltpu.VMEM((tm, D_MODEL), BF16), pltpu.VMEM((tm, D_MODEL), F32)],
        compiler_params=pltpu.CompilerParams(dimension_semantics=("arbitrary",),
                                             vmem_limit_bytes=VMEM_LIMIT),
        name="ffn1_inproj",
    )(x, *params)


def _outproj_ffn2(l, x, yr, ya, nr, wo, n2, wg, wu, wd, nf, final_norm):
    t = x.shape[0]
    tm = TOKEN_TILE
    params = (nr, wo, n2, wg, wu, wd)
    return pl.pallas_call(
        functools.partial(_outproj_ffn2_kernel, final_norm=final_norm),
        grid=(t // tm,),
        in_specs=[_rows(D_MODEL), _rows(LRU_WIDTH), _rows(ATT_WIDTH)] + [_layer(p, l) for p in params]
                 + [pl.BlockSpec(nf.shape, lambda i: (0, 0), pipeline_mode=pl.Buffered(1))],
        out_specs=_rows(D_MODEL),
        out_shape=jax.ShapeDtypeStruct((t, D_MODEL), F32),
        scratch_shapes=[pltpu.VMEM((tm, D_MODEL), BF16), pltpu.VMEM((tm, D_MODEL), F32)],
        compiler_params=pltpu.CompilerParams(dimension_semantics=("arbitrary",),
                                             vmem_limit_bytes=VMEM_LIMIT),
        name="outproj_ffn2",
    )(x, yr, ya, *params, nf)


def _rec_kernel(xr_ref, gate_ref, cw_ref, cb_ref, wg_ref, bg_ref, lam_ref, o_ref,
                xpad, a_f, u_f, a_b, u_b, h_f, h_b, *, seq):
    n_chunks = seq // REC_CHUNK
    n_steps = SEG_PITCH
    pad_rows = SCAN_SEGS * SEG_PITCH - seq

    zeros_pad = jnp.zeros((PAD_ROWS, LANES), F32)
    xpad[0:PAD_ROWS, :] = zeros_pad
    xpad[PAD_ROWS + seq:2 * PAD_ROWS + seq, :] = zeros_pad
    xpad[PAD_ROWS:PAD_ROWS + seq, :] = xr_ref[...]

    for a_ref, u_ref in ((a_f, u_f), (a_b, u_b)):
        a_ref[seq:seq + pad_rows, :] = jnp.ones((pad_rows, LANES), F32)
        u_ref[seq:seq + pad_rows, :] = jnp.zeros((pad_rows, LANES), F32)

    lam = lam_ref[...]
    neg_softplus = -(jnp.maximum(-lam, 0.0) + jnp.log1p(jnp.exp(-jnp.abs(lam))))
    half_log2_a = (0.5 * LRU_C * LOG2E) * neg_softplus

    cw = cw_ref[...]
    cb = cb_ref[...]
    for c in range(n_chunks):
        base = PAD_ROWS + c * REC_CHUNK
        xc = cb
        for tap in range(CONV_WIDTH):
            lo = base + tap - CONV_WIDTH // 2
            xc = xc + cw[tap:tap + 1, :] * xpad[lo:lo + REC_CHUNK, :]
        th = jnp.tanh(jnp.dot(xc.astype(BF16), wg_ref[...], preferred_element_type=F32) + bg_ref[...])
        half_xc = 0.5 * xc
        rows = slice(c * REC_CHUNK, (c + 1) * REC_CHUNK)
        for d, (a_ref, u_ref) in enumerate(((a_f, u_f), (a_b, u_b))):
            th_r = th[:, (2 * d) * LANES:(2 * d + 1) * LANES]
            th_i = th[:, (2 * d + 1) * LANES:(2 * d + 2) * LANES]
            scale = half_log2_a[d:d + 1, :]
            a = jnp.exp2(scale * th_r + scale)
            v = 1.0 - a * a
            root = jnp.where(v > 0.0, v * lax.rsqrt(v), 0.0)
            a_ref[rows, :] = a
            u_ref[rows, :] = root * (th_i * half_xc + half_xc)

    def seg_rows(j):
        return pl.ds(j, SCAN_SEGS, stride=SEG_PITCH)

    zeros = jnp.zeros((SCAN_SEGS, LANES), F32)
    ones = jnp.ones((SCAN_SEGS, LANES), F32)

    def segment_maps(j, carry):
        hf, pf, hb, pb = carry
        jb = n_steps - 1 - j
        af = a_f[seg_rows(j), :]
        ab = a_b[seg_rows(jb), :]
        hf = af * hf + u_f[seg_rows(j), :]
        hb = ab * hb + u_b[seg_rows(jb), :]
        return hf, pf * af, hb, pb * ab

    hf, pf, hb, pb = lax.fori_loop(0, n_steps, segment_maps, (zeros, ones, zeros, ones), unroll=4)

    seg = lax.broadcasted_iota(jnp.int32, (SCAN_SEGS, LANES), 0)

    def entering(h, p, toward_higher):
        d = 1
        while d < SCAN_SEGS:
            shift = d if toward_higher else SCAN_SEGS - d
            have = (seg >= d) if toward_higher else (seg < SCAN_SEGS - d)
            h = jnp.where(have, h + p * pltpu.roll(h, shift, 0), h)
            p = jnp.where(have, p * pltpu.roll(p, shift, 0), p)
            d *= 2
        one_step = 1 if toward_higher else SCAN_SEGS - 1
        edge = (seg == 0) if toward_higher else (seg == SCAN_SEGS - 1)
        return jnp.where(edge, 0.0, pltpu.roll(h, one_step, 0))

    hf0 = entering(hf, pf, True)
    hb0 = entering(hb, pb, False)

    def true_states(j, carry):
        hf, hb = carry
        jb = n_steps - 1 - j
        hf = a_f[seg_rows(j), :] * hf + u_f[seg_rows(j), :]
        hb = a_b[seg_rows(jb), :] * hb + u_b[seg_rows(jb), :]
        h_f[seg_rows(j), :] = hf
        h_b[seg_rows(jb), :] = hb
        return hf, hb

    lax.fori_loop(0, n_steps, true_states, (hf0, hb0), unroll=4)

    for c in range(n_chunks):
        rows = slice(c * REC_CHUNK, (c + 1) * REC_CHUNK)
        o_ref[rows, :] = gate_ref[rows, :] * (h_f[rows, :] + h_b[rows, :])


def _rec(l, xr, gate, cw, cb, wg, bg, lam, batch, seq):
    n_slabs = LRU_WIDTH // LANES
    assert SCAN_SEGS * SEG_PITCH >= seq > (SCAN_SEGS - 1) * SEG_PITCH and seq % REC_CHUNK == 0
    slab = pl.BlockSpec((seq, LANES), lambda b, s: (b, s))
    scan_rows = SCAN_SEGS * SEG_PITCH
    return pl.pallas_call(
        functools.partial(_rec_kernel, seq=seq),
        grid=(batch, n_slabs),
        in_specs=[slab, slab,
                  pl.BlockSpec((None, CONV_WIDTH, LANES), lambda b, s: (l, 0, s)),
                  pl.BlockSpec((None, 1, LANES), lambda b, s: (l, 0, s)),
                  pl.BlockSpec((None, None, LANES, 4 * LANES), lambda b, s: (l, s, 0, 0)),
                  pl.BlockSpec((None, None, 1, 4 * LANES), lambda b, s: (l, s, 0, 0)),
                  pl.BlockSpec((None, 2, LANES), lambda b, s: (l, 0, s))],
        out_specs=slab,
        out_shape=jax.ShapeDtypeStruct((batch * seq, LRU_WIDTH), F32),
        scratch_shapes=[pltpu.VMEM((seq + 2 * PAD_ROWS, LANES), F32)]
                       + [pltpu.VMEM((scan_rows, LANES), F32)] * 6,
        compiler_params=pltpu.CompilerParams(dimension_semantics=("arbitrary", "arbitrary"),
                                             vmem_limit_bytes=VMEM_LIMIT),
        name="rg_lru",
    )(xr, gate, cw, cb, wg, bg, lam)


def _bias_kernel(buckets_ref, table_ref, o_ref):
    buckets = buckets_ref[...]
    for h in range(N_HEADS):
        acc = jnp.zeros(buckets.shape, F32)
        for b in range(N_BUCKETS):
            acc = jnp.where(buckets == b, table_ref[b, h] * LOG2E, acc)
        g, p, e = h // KV_GROUP, (h % KV_GROUP) // 2, h % 2
        o_ref[2 * g + e, p * BLOCK:(p + 1) * BLOCK, :] = acc


def _rel_bias_table(buckets, rel_bias):
    return pl.pallas_call(
        _bias_kernel,
        in_specs=[pl.BlockSpec(memory_space=pltpu.VMEM), pl.BlockSpec(memory_space=pltpu.SMEM)],
        out_specs=pl.BlockSpec(memory_space=pltpu.VMEM),
        out_shape=jax.ShapeDtypeStruct((N_HEADS // 2, 2 * BLOCK, 3 * BLOCK), F32),
        name="rel_bias",
    )(buckets, rel_bias)


ATT_SETUP_ROWS = 512
HALF = LANES // 2


def _attn_kernel(q_ref, k_ref, v_ref, bias_ref, sink_ref, gn_ref, o_ref, kvar, vvar, *, seq, layer):
    nb = seq // BLOCK
    lo_half = lax.broadcasted_iota(jnp.int32, (ATT_SETUP_ROWS, LANES), 1) < HALF
    zeros_blk = jnp.zeros((BLOCK, LANES), BF16)
    for u in range(2 * N_KV_HEADS):
        for ref in (kvar, vvar):
            ref[u, 0:BLOCK, :] = zeros_blk
            ref[u, BLOCK + seq:2 * BLOCK + seq, :] = zeros_blk
    for r in range(seq // ATT_SETUP_ROWS):
        src = slice(r * ATT_SETUP_ROWS, (r + 1) * ATT_SETUP_ROWS)
        dst = slice(BLOCK + r * ATT_SETUP_ROWS, BLOCK + (r + 1) * ATT_SETUP_ROWS)
        kc = k_ref[src, :].astype(F32)
        vc = v_ref[src, :].astype(F32)
        kr = pltpu.roll(kc, HALF, 1)
        vr = pltpu.roll(vc, HALF, 1)
        kvar[0, dst, :] = jnp.where(lo_half, kc, 0.0).astype(BF16)
        kvar[1, dst, :] = jnp.where(lo_half, 0.0, kr).astype(BF16)
        kvar[2, dst, :] = jnp.where(lo_half, kr, 0.0).astype(BF16)
        kvar[3, dst, :] = jnp.where(lo_half, 0.0, kc).astype(BF16)
        vvar[0, dst, :] = jnp.where(lo_half, vc, 1.0).astype(BF16)
        vvar[1, dst, :] = jnp.where(lo_half, 1.0, vr).astype(BF16)
        vvar[2, dst, :] = jnp.where(lo_half, vr, 1.0).astype(BF16)
        vvar[3, dst, :] = jnp.where(lo_half, 1.0, vc).astype(BF16)

    assert BLOCK == WINDOW == LANES
    t_idx = lax.broadcasted_iota(jnp.int32, (2 * BLOCK, LANES), 0) % BLOCK
    lane_minus_t = lax.broadcasted_iota(jnp.int32, (2 * BLOCK, LANES), 1) - t_idx
    first_head = lax.broadcasted_iota(jnp.int32, (2 * BLOCK, 1), 0) < BLOCK
    lo_out = lax.broadcasted_iota(jnp.int32, (BLOCK, LANES), 1) < HALF
    gn = gn_ref[...]
    units = [(g, e) for g in range(N_KV_HEADS) for e in range(2)]
    never = 2 * LANES

    def block(n, carry):
        start = pl.multiple_of(n * BLOCK, BLOCK)
        prev_min = jnp.where(n == 0, never, 0)
        next_max = jnp.where(n == nb - 1, -never, 0)
        qblk = q_ref[pl.ds(start, BLOCK), :]
        scores = []
        for g, e in units:
            lhs = jnp.concatenate([qblk[:, (2 * g) * LANES:(2 * g + 1) * LANES],
                                   qblk[:, (2 * g + 1) * LANES:(2 * g + 2) * LANES]], axis=0)
            kwin = kvar[2 * g + e, pl.ds(start, 3 * BLOCK), :]
            scores.append(lax.dot_general(lhs, kwin, (((1,), (1,)), ((), ())),
                                          preferred_element_type=F32))
        probs, sink_mass = [], []
        for (g, e), s in zip(units, scores):
            u = 2 * g + e
            s_prev = jnp.where(lane_minus_t >= prev_min, s[:, :BLOCK] + bias_ref[u, :, :BLOCK], NEG_INF)
            s_own = s[:, BLOCK:2 * BLOCK] + bias_ref[u, :, BLOCK:2 * BLOCK]
            s_next = jnp.where(lane_minus_t <= next_max, s[:, 2 * BLOCK:] + bias_ref[u, :, 2 * BLOCK:], NEG_INF)
            sink = jnp.where(first_head, sink_ref[layer, 4 * g + e], sink_ref[layer, 4 * g + 2 + e]) * LOG2E
            row_max = jnp.max(jnp.maximum(jnp.maximum(s_prev, s_own), s_next), axis=-1, keepdims=True)
            m = jnp.maximum(row_max, sink)
            p = jnp.concatenate([jnp.exp2(s_prev - m), jnp.exp2(s_own - m), jnp.exp2(s_next - m)], axis=-1)
            probs.append(p.astype(BF16))
            sink_mass.append(jnp.exp2(sink - m))
        outs = []
        for (g, e), p in zip(units, probs):
            vwin = vvar[2 * g + e, pl.ds(start, 3 * BLOCK), :]
            outs.append(jnp.dot(p, vwin, preferred_element_type=F32))
        chunks = []
        for g in range(N_KV_HEADS):
            for p in range(2):
                rows = slice(p * BLOCK, (p + 1) * BLOCK)
                even, odd = outs[2 * g][rows], outs[2 * g + 1][rows]
                num = jnp.where(lo_out, even, odd)
                den = (pltpu.roll(jnp.where(lo_out, odd, even), HALF, 1)
                       + jnp.where(lo_out, sink_mass[2 * g][rows], sink_mass[2 * g + 1][rows]))
                chunks.append(num / den)
        y = jnp.concatenate(chunks, axis=-1)
        o_ref[pl.ds(start, BLOCK), :] = _rms(y, gn).astype(BF16)
        return carry

    lax.fori_loop(0, nb, block, 0, unroll=4)


def _attn(l, q, k, v, bias, sink, gn, batch, seq):
    assert seq % ATT_SETUP_ROWS == 0

    def seq_rows(width):
        return pl.BlockSpec((seq, width), lambda b: (b, 0))

    return pl.pallas_call(
        functools.partial(_attn_kernel, seq=seq, layer=l),
        grid=(batch,),
        in_specs=[seq_rows(ATT_WIDTH), seq_rows(KV_WIDTH), seq_rows(KV_WIDTH),
                  pl.BlockSpec(bias.shape, lambda b: (0, 0, 0), pipeline_mode=pl.Buffered(1)),
                  pl.BlockSpec(memory_space=pltpu.SMEM),
                  _layer(gn, l)],
        out_specs=seq_rows(ATT_WIDTH),
        out_shape=jax.ShapeDtypeStruct((batch * seq, ATT_WIDTH), BF16),
        scratch_shapes=[pltpu.VMEM((2 * N_KV_HEADS, seq + 2 * BLOCK, LANES), BF16)] * 2,
        compiler_params=pltpu.CompilerParams(dimension_semantics=("arbitrary",),
                                             vmem_limit_bytes=VMEM_LIMIT),
        name="windowed_gqa",
    )(q, k, v, bias, sink, gn)


def _t5_bucket_index():
    t = jnp.arange(BLOCK)[:, None]
    j = jnp.arange(3 * BLOCK)[None, :]
    rel = j - BLOCK - t
    half = N_BUCKETS // 2
    max_exact = half // 2
    ret = (rel > 0).astype(jnp.int32) * half
    n = jnp.abs(rel)
    n_f = jnp.maximum(n, 1).astype(jnp.float32)
    large = max_exact + (jnp.log(n_f / max_exact) / math.log(MAX_DISTANCE / max_exact)
                         * (half - max_exact)).astype(jnp.int32)
    large = jnp.minimum(large, half - 1)
    return ret + jnp.where(n < max_exact, n, large)


def _gate_weights(w_a, b_a, w_x, b_x):
    depth = w_a.shape[0]
    n_slabs = LRU_WIDTH // LANES
    per_slab = LANES // LRU_BLOCK_W

    def slab_blockdiag(w):
        w = w.reshape(depth, n_slabs, per_slab, LRU_BLOCK_W, LRU_BLOCK_W)
        eye = jnp.eye(per_slab, dtype=w.dtype)
        full = jnp.einsum('lspcd,pq->lspcqd', w, eye)
        return full.reshape(depth, n_slabs, LANES, LANES)

    mats = [slab_blockdiag(w[:, d]) for d in range(2) for w in (w_a, w_x)]
    wg = (0.5 * jnp.concatenate(mats, axis=-1)).astype(BF16)
    biases = [b[:, d].reshape(depth, n_slabs, 1, LANES) for d in range(2) for b in (b_a, b_x)]
    bg = 0.5 * jnp.concatenate(biases, axis=-1)
    return wg, bg


def kernel(x, ffn1_norm, ffn1_w_gate, ffn1_w_up, ffn1_w_down, mix_norm, w_in, conv_w, conv_b, lru_w_a, lru_b_a, lru_w_x, lru_b_x, lru_lambda, attn_sink, rel_bias, lru_out_norm, attn_out_norm, w_out, ffn2_norm, ffn2_w_gate, ffn2_w_up, ffn2_w_down, final_norm):
    batch, seq, d_model = x.shape
    depth = ffn1_norm.shape[0]
    assert d_model == D_MODEL and seq % BLOCK == 0 and (batch * seq) % TOKEN_TILE == 0
    xt = x.reshape(batch * seq, D_MODEL)
    bias = _rel_bias_table(_t5_bucket_index(), rel_bias)
    rows = lambda p: p.reshape(depth, 1, -1)
    bf16 = lambda w: w.astype(BF16)
    wg1, wu1, wd1 = bf16(ffn1_w_gate), bf16(ffn1_w_up), bf16(ffn1_w_down)
    wg2, wu2, wd2 = bf16(ffn2_w_gate), bf16(ffn2_w_up), bf16(ffn2_w_down)
    win, wout = bf16(w_in), bf16(w_out)
    gate_w, gate_b = _gate_weights(lru_w_a, lru_b_a, lru_w_x, lru_b_x)
    n1, nm, n2 = rows(ffn1_norm), rows(mix_norm), rows(ffn2_norm)
    nr, na, cb = rows(lru_out_norm), rows(attn_out_norm), rows(conv_b)
    nf = final_norm.reshape(1, -1)
    for l in range(depth):
        xt, xr, gate, q, k, v = _ffn1_inproj(l, xt, n1, wg1, wu1, wd1, nm, win)
        y_rec = _rec(l, xr, gate, conv_w, cb, gate_w, gate_b, lru_lambda, batch, seq)
        y_att = _attn(l, q, k, v, bias, attn_sink, na, batch, seq)
        xt = _outproj_ffn2(l, xt, y_rec, y_att, nr, wout, n2, wg2, wu2, wd2, nf,
                           final_norm=(l == depth - 1))
    return xt.reshape(batch, seq, D_MODEL)
```

```python
import functools
import math

import jax
import jax.numpy as jnp
from jax import lax
from jax.experimental import pallas as pl
from jax.experimental.pallas import tpu as pltpu

F32 = jnp.float32
BF16 = jnp.bfloat16

D_MODEL = 1024
LRU_WIDTH = 512
LRU_BLOCK_W = 64
LRU_C = 8.0
CONV_WIDTH = 4
N_HEADS = 8
N_KV_HEADS = 2
KV_GROUP = N_HEADS // N_KV_HEADS
HEAD_DIM = 64
ATT_WIDTH = N_HEADS * HEAD_DIM
KV_WIDTH = N_KV_HEADS * HEAD_DIM
WINDOW = 128
BLOCK = 128
N_BUCKETS = 32
MAX_DISTANCE = 128
D_FF = 2816
FFN_RES = 0.5
EPS = 1e-6
NEG_INF = -1e30
D_IN = 2 * LRU_WIDTH + ATT_WIDTH + 2 * KV_WIDTH
LOG2E = math.log2(math.e)

LANES = 128
SUBLANES = 8
FF_CHUNK = 256
N_FF_CHUNKS = D_FF // FF_CHUNK
FFN1_TILE = 512
FFN2_TILE = 1024
VMEM_LIMIT = 56 * 1024 * 1024

SCAN_SEGS = 32
SCAN_VREGS = SCAN_SEGS // SUBLANES
SEG_PITCH = 129
REC_CHUNK = 512
PAD_ROWS = SUBLANES


def _rms(x, g):
    return x * lax.rsqrt(jnp.mean(x * x, axis=-1, keepdims=True) + EPS) * g


def _sigmoid(x):
    return 0.5 * jnp.tanh(0.5 * x) + 0.5


def _swiglu_into(h_ref, wg_ref, wu_ref, wd_ref, acc_ref):
    for c in range(N_FF_CHUNKS):
        cols = slice(c * FF_CHUNK, (c + 1) * FF_CHUNK)
        g = jnp.dot(h_ref[...], wg_ref[:, cols], preferred_element_type=F32)
        u = jnp.dot(h_ref[...], wu_ref[:, cols], preferred_element_type=F32)
        act = (g * _sigmoid(g) * u).astype(BF16)
        part = jnp.dot(act, wd_ref[cols, :], preferred_element_type=F32)
        if c == 0:
            acc_ref[...] = part
        else:
            acc_ref[...] += part


N_CAST = 4


def _cast_blocks(src_refs, dst_refs):
    for src, dst in zip(src_refs, dst_refs):
        dst[...] = src[...].astype(BF16)


def _ffn1_inproj_kernel(x_ref, n1_ref, wg_ref, wu_ref, wd_ref, nm_ref, win_ref, *rest):
    cast_src, rest = rest[:N_CAST], rest[N_CAST:]
    xo_ref, xr_ref, gate_ref, q_ref, k_ref, v_ref = rest[:6]
    cast_dst, (h_ref, acc_ref) = rest[6:6 + N_CAST], rest[6 + N_CAST:]
    _cast_blocks(cast_src, cast_dst)
    x = x_ref[...]
    h_ref[...] = _rms(x, n1_ref[...]).astype(BF16)
    _swiglu_into(h_ref, wg_ref, wu_ref, wd_ref, acc_ref)
    x1 = x + FFN_RES * acc_ref[...]
    xo_ref[...] = x1
    hm = _rms(x1, nm_ref[...]).astype(BF16)
    proj = jnp.dot(hm, win_ref[...], preferred_element_type=F32)
    xr_ref[...] = proj[:, :LRU_WIDTH]
    gate_ref[...] = jax.nn.gelu(proj[:, LRU_WIDTH:2 * LRU_WIDTH])
    o = 2 * LRU_WIDTH
    q_ref[...] = (proj[:, o:o + ATT_WIDTH] * (HEAD_DIM ** -0.5 * LOG2E)).astype(BF16)
    o += ATT_WIDTH
    k_ref[...] = proj[:, o:o + KV_WIDTH].astype(BF16)
    o += KV_WIDTH
    v_ref[...] = proj[:, o:o + KV_WIDTH].astype(BF16)


def _outproj_ffn2_kernel(x_ref, yr_ref, ya_ref, nr_ref, wo_ref, n2_ref, wg_ref, wu_ref, wd_ref, nf_ref,
                         xo_ref, h_ref, acc_ref, *, final_norm):
    yr = _rms(yr_ref[...], nr_ref[...]).astype(BF16)
    x1 = (x_ref[...]
          + jnp.dot(yr, wo_ref[:LRU_WIDTH, :], preferred_element_type=F32)
          + jnp.dot(ya_ref[...], wo_ref[LRU_WIDTH:, :], preferred_element_type=F32))
    h_ref[...] = _rms(x1, n2_ref[...]).astype(BF16)
    _swiglu_into(h_ref, wg_ref, wu_ref, wd_ref, acc_ref)
    x2 = x1 + FFN_RES * acc_ref[...]
    if final_norm:
        x2 = _rms(x2, nf_ref[...])
    xo_ref[...] = x2


def _layer(stacked, l):
    assert stacked.ndim == 3
    return pl.BlockSpec((None,) + stacked.shape[1:], lambda i: (l, 0, 0), pipeline_mode=pl.Buffered(1))


def _whole(arr):
    assert arr.ndim == 2
    return pl.BlockSpec(arr.shape, lambda *_: (0, 0), pipeline_mode=pl.Buffered(1))


def _rows(width, tm):
    return pl.BlockSpec((tm, width), lambda i: (i, 0))


def _cast_plan(stacks, l, n_steps, step_of):
    bf16_rows = 2 * SUBLANES
    in_specs, out_specs, out_shapes = [], [], []
    for w in stacks:
        _, r, c = w.shape
        n_blocks = n_steps
        while r % n_blocks or (r // n_blocks) % bf16_rows:
            n_blocks //= 2
        assert n_blocks >= 1 and n_steps % n_blocks == 0
        rep = n_steps // n_blocks
        in_specs.append(pl.BlockSpec((None, r // n_blocks, c), lambda *g, rep=rep: (l, step_of(*g) // rep, 0)))
        out_specs.append(pl.BlockSpec((r // n_blocks, c), lambda *g, rep=rep: (step_of(*g) // rep, 0)))
        out_shapes.append(jax.ShapeDtypeStruct((r, c), BF16))
    return in_specs, out_specs, out_shapes


def _ffn1_inproj(l, x, n1, wg, wu, wd, nm, win, cast_stacks):
    t = x.shape[0]
    tm = FFN1_TILE
    out_shape = [
        jax.ShapeDtypeStruct((t, D_MODEL), F32),
        jax.ShapeDtypeStruct((t, LRU_WIDTH), F32),
        jax.ShapeDtypeStruct((t, LRU_WIDTH), F32),
        jax.ShapeDtypeStruct((t, ATT_WIDTH), BF16),
        jax.ShapeDtypeStruct((t, KV_WIDTH), BF16),
        jax.ShapeDtypeStruct((t, KV_WIDTH), BF16),
    ]
    out_specs = [_rows(D_MODEL, tm), _rows(LRU_WIDTH, tm), _rows(LRU_WIDTH, tm), _rows(ATT_WIDTH, tm),
                 _rows(KV_WIDTH, tm), _rows(KV_WIDTH, tm)]
    cast_in, cast_out, cast_shapes = _cast_plan(cast_stacks, l, t // tm, lambda i: i)
    outs = pl.pallas_call(
        _ffn1_inproj_kernel,
        grid=(t // tm,),
        in_specs=[_rows(D_MODEL, tm), _layer(n1, l), _whole(wg), _whole(wu), _whole(wd), _layer(nm, l),
                  _whole(win)] + cast_in,
        out_specs=out_specs + cast_out,
        out_shape=out_shape + cast_shapes,
        scratch_shapes=[pltpu.VMEM((tm, D_MODEL), BF16), pltpu.VMEM((tm, D_MODEL), F32)],
        compiler_params=pltpu.CompilerParams(dimension_semantics=("arbitrary",),
                                             vmem_limit_bytes=VMEM_LIMIT),
        name="ffn1_inproj",
    )(x, n1, wg, wu, wd, nm, win, *cast_stacks)
    return outs[:6], outs[6:]


def _outproj_ffn2(l, x, yr, ya, nr, wo, n2, wg, wu, wd, nf, final_norm):
    t = x.shape[0]
    tm = FFN2_TILE
    return pl.pallas_call(
        functools.partial(_outproj_ffn2_kernel, final_norm=final_norm),
        grid=(t // tm,),
        in_specs=[_rows(D_MODEL, tm), _rows(LRU_WIDTH, tm), _rows(ATT_WIDTH, tm), _layer(nr, l), _whole(wo),
                  _layer(n2, l), _whole(wg), _whole(wu), _whole(wd), _whole(nf)],
        out_specs=_rows(D_MODEL, tm),
        out_shape=jax.ShapeDtypeStruct((t, D_MODEL), F32),
        scratch_shapes=[pltpu.VMEM((tm, D_MODEL), BF16), pltpu.VMEM((tm, D_MODEL), F32)],
        compiler_params=pltpu.CompilerParams(dimension_semantics=("arbitrary",),
                                             vmem_limit_bytes=VMEM_LIMIT),
        name="outproj_ffn2",
    )(x, yr, ya, nr, wo, n2, wg, wu, wd, nf)


def _rec_kernel(xr_ref, gate_ref, cw_ref, cb_ref, wg_ref, bg_ref, lam_ref, *rest, seq, n_cast):
    cast_src, o_ref = rest[:n_cast], rest[n_cast]
    cast_dst = rest[n_cast + 1:2 * n_cast + 1]
    xpad, a_f, u_f, a_b, u_b, h_f, h_b = rest[2 * n_cast + 1:]
    _cast_blocks(cast_src, cast_dst)
    n_chunks = seq // REC_CHUNK
    n_steps = SEG_PITCH
    pad_rows = SCAN_SEGS * SEG_PITCH - seq

    zeros_pad = jnp.zeros((PAD_ROWS, LANES), F32)
    xpad[0:PAD_ROWS, :] = zeros_pad
    xpad[PAD_ROWS + seq:2 * PAD_ROWS + seq, :] = zeros_pad
    xpad[PAD_ROWS:PAD_ROWS + seq, :] = xr_ref[...]

    for a_ref, u_ref in ((a_f, u_f), (a_b, u_b)):
        a_ref[seq:seq + pad_rows, :] = jnp.ones((pad_rows, LANES), F32)
        u_ref[seq:seq + pad_rows, :] = jnp.zeros((pad_rows, LANES), F32)

    lam = lam_ref[...]
    neg_softplus = -(jnp.maximum(-lam, 0.0) + jnp.log1p(jnp.exp(-jnp.abs(lam))))
    half_log2_a = (0.5 * LRU_C * LOG2E) * neg_softplus

    cw = cw_ref[...]
    cb = cb_ref[...]
    for c in range(n_chunks):
        base = PAD_ROWS + c * REC_CHUNK
        xc = cb
        for tap in range(CONV_WIDTH):
            lo = base + tap - CONV_WIDTH // 2
            xc = xc + cw[tap:tap + 1, :] * xpad[lo:lo + REC_CHUNK, :]
        th = jnp.tanh(jnp.dot(xc.astype(BF16), wg_ref[...], preferred_element_type=F32) + bg_ref[...])
        half_xc = 0.5 * xc
        rows = slice(c * REC_CHUNK, (c + 1) * REC_CHUNK)
        for d, (a_ref, u_ref) in enumerate(((a_f, u_f), (a_b, u_b))):
            th_r = th[:, (2 * d) * LANES:(2 * d + 1) * LANES]
            th_i = th[:, (2 * d + 1) * LANES:(2 * d + 2) * LANES]
            scale = half_log2_a[d:d + 1, :]
            a = jnp.exp2(scale * th_r + scale)
            v = 1.0 - a * a
            root = jnp.where(v > 0.0, v * lax.rsqrt(v), 0.0)
            a_ref[rows, :] = a
            u_ref[rows, :] = root * (th_i * half_xc + half_xc)

    def seg_rows(j):
        return pl.ds(j, SCAN_SEGS, stride=SEG_PITCH)

    zeros = jnp.zeros((SCAN_SEGS, LANES), F32)
    ones = jnp.ones((SCAN_SEGS, LANES), F32)

    def segment_maps(j, carry):
        hf, pf, hb, pb = carry
        jb = n_steps - 1 - j
        af = a_f[seg_rows(j), :]
        ab = a_b[seg_rows(jb), :]
        hf = af * hf + u_f[seg_rows(j), :]
        hb = ab * hb + u_b[seg_rows(jb), :]
        return hf, pf * af, hb, pb * ab

    hf, pf, hb, pb = lax.fori_loop(0, n_steps, segment_maps, (zeros, ones, zeros, ones), unroll=4)

    seg = lax.broadcasted_iota(jnp.int32, (SCAN_SEGS, LANES), 0)

    def entering(h, p, toward_higher):
        d = 1
        while d < SCAN_SEGS:
            shift = d if toward_higher else SCAN_SEGS - d
            have = (seg >= d) if toward_higher else (seg < SCAN_SEGS - d)
            h = jnp.where(have, h + p * pltpu.roll(h, shift, 0), h)
            p = jnp.where(have, p * pltpu.roll(p, shift, 0), p)
            d *= 2
        one_step = 1 if toward_higher else SCAN_SEGS - 1
        edge = (seg == 0) if toward_higher else (seg == SCAN_SEGS - 1)
        return jnp.where(edge, 0.0, pltpu.roll(h, one_step, 0))

    hf0 = entering(hf, pf, True)
    hb0 = entering(hb, pb, False)

    def true_states(j, carry):
        hf, hb = carry
        jb = n_steps - 1 - j
        hf = a_f[seg_rows(j), :] * hf + u_f[seg_rows(j), :]
        hb = a_b[seg_rows(jb), :] * hb + u_b[seg_rows(jb), :]
        h_f[seg_rows(j), :] = hf
        h_b[seg_rows(jb), :] = hb
        return hf, hb

    lax.fori_loop(0, n_steps, true_states, (hf0, hb0), unroll=4)

    for c in range(n_chunks):
        rows = slice(c * REC_CHUNK, (c + 1) * REC_CHUNK)
        o_ref[rows, :] = gate_ref[rows, :] * (h_f[rows, :] + h_b[rows, :])


def _rec(l, xr, gate, cw, cb, wg, bg, lam, batch, seq, cast_stacks, cast_layer):
    n_slabs = LRU_WIDTH // LANES
    assert SCAN_SEGS * SEG_PITCH >= seq > (SCAN_SEGS - 1) * SEG_PITCH and seq % REC_CHUNK == 0
    slab = pl.BlockSpec((seq, LANES), lambda b, s: (b, s))
    scan_rows = SCAN_SEGS * SEG_PITCH
    cast_in, cast_out, cast_shapes = _cast_plan(cast_stacks, cast_layer, batch * n_slabs,
                                                lambda b, s: b * n_slabs + s)
    outs = pl.pallas_call(
        functools.partial(_rec_kernel, seq=seq, n_cast=len(cast_stacks)),
        grid=(batch, n_slabs),
        in_specs=[slab, slab,
                  pl.BlockSpec((None, CONV_WIDTH, LANES), lambda b, s: (l, 0, s)),
                  pl.BlockSpec((None, 1, LANES), lambda b, s: (l, 0, s)),
                  pl.BlockSpec((None, None, LANES, 4 * LANES), lambda b, s: (l, s, 0, 0)),
                  pl.BlockSpec((None, None, 1, 4 * LANES), lambda b, s: (l, s, 0, 0)),
                  pl.BlockSpec((None, 2, LANES), lambda b, s: (l, 0, s))] + cast_in,
        out_specs=[slab] + cast_out,
        out_shape=[jax.ShapeDtypeStruct((batch * seq, LRU_WIDTH), F32)] + cast_shapes,
        scratch_shapes=[pltpu.VMEM((seq + 2 * PAD_ROWS, LANES), F32)]
                       + [pltpu.VMEM((scan_rows, LANES), F32)] * 6,
        compiler_params=pltpu.CompilerParams(dimension_semantics=("arbitrary", "arbitrary"),
                                             vmem_limit_bytes=VMEM_LIMIT),
        name="rg_lru",
    )(xr, gate, cw, cb, wg, bg, lam, *cast_stacks)
    return outs[0], outs[1:]


def _bias_kernel(buckets_ref, table_ref, o_ref):
    buckets = buckets_ref[...]
    for h in range(N_HEADS):
        acc = jnp.zeros(buckets.shape, F32)
        for b in range(N_BUCKETS):
            acc = jnp.where(buckets == b, table_ref[b, h] * LOG2E, acc)
        g, p, e = h // KV_GROUP, (h % KV_GROUP) // 2, h % 2
        o_ref[2 * g + e, p * BLOCK:(p + 1) * BLOCK, :] = acc


def _rel_bias_table(buckets, rel_bias):
    return pl.pallas_call(
        _bias_kernel,
        in_specs=[pl.BlockSpec(memory_space=pltpu.VMEM), pl.BlockSpec(memory_space=pltpu.SMEM)],
        out_specs=pl.BlockSpec(memory_space=pltpu.VMEM),
        out_shape=jax.ShapeDtypeStruct((N_HEADS // 2, 2 * BLOCK, 3 * BLOCK), F32),
        name="rel_bias",
    )(buckets, rel_bias)


ATT_SETUP_ROWS = 512
HALF = LANES // 2


def _attn_kernel(q_ref, k_ref, v_ref, bias_ref, sink_ref, gn_ref, o_ref, kvar, vvar, *, seq, layer):
    nb = seq // BLOCK
    lo_half = lax.broadcasted_iota(jnp.int32, (ATT_SETUP_ROWS, LANES), 1) < HALF
    zeros_blk = jnp.zeros((BLOCK, LANES), BF16)
    for u in range(2 * N_KV_HEADS):
        for ref in (kvar, vvar):
            ref[u, 0:BLOCK, :] = zeros_blk
            ref[u, BLOCK + seq:2 * BLOCK + seq, :] = zeros_blk
    for r in range(seq // ATT_SETUP_ROWS):
        src = slice(r * ATT_SETUP_ROWS, (r + 1) * ATT_SETUP_ROWS)
        dst = slice(BLOCK + r * ATT_SETUP_ROWS, BLOCK + (r + 1) * ATT_SETUP_ROWS)
        kc = k_ref[src, :].astype(F32)
        vc = v_ref[src, :].astype(F32)
        kr = pltpu.roll(kc, HALF, 1)
        vr = pltpu.roll(vc, HALF, 1)
        kvar[0, dst, :] = jnp.where(lo_half, kc, 0.0).astype(BF16)
        kvar[1, dst, :] = jnp.where(lo_half, 0.0, kr).astype(BF16)
        kvar[2, dst, :] = jnp.where(lo_half, kr, 0.0).astype(BF16)
        kvar[3, dst, :] = jnp.where(lo_half, 0.0, kc).astype(BF16)
        vvar[0, dst, :] = jnp.where(lo_half, vc, 1.0).astype(BF16)
        vvar[1, dst, :] = jnp.where(lo_half, 1.0, vr).astype(BF16)
        vvar[2, dst, :] = jnp.where(lo_half, vr, 1.0).astype(BF16)
        vvar[3, dst, :] = jnp.where(lo_half, 1.0, vc).astype(BF16)

    assert BLOCK == WINDOW == LANES
    t_idx = lax.broadcasted_iota(jnp.int32, (2 * BLOCK, LANES), 0) % BLOCK
    lane_minus_t = lax.broadcasted_iota(jnp.int32, (2 * BLOCK, LANES), 1) - t_idx
    first_head = lax.broadcasted_iota(jnp.int32, (2 * BLOCK, 1), 0) < BLOCK
    lo_out = lax.broadcasted_iota(jnp.int32, (BLOCK, LANES), 1) < HALF
    gn = gn_ref[...]
    units = [(g, e) for g in range(N_KV_HEADS) for e in range(2)]
    never = 2 * LANES

    def block(n, carry):
        start = pl.multiple_of(n * BLOCK, BLOCK)
        prev_min = jnp.where(n == 0, never, 0)
        next_max = jnp.where(n == nb - 1, -never, 0)
        qblk = q_ref[pl.ds(start, BLOCK), :]
        scores = []
        for g, e in units:
            lhs = jnp.concatenate([qblk[:, (2 * g) * LANES:(2 * g + 1) * LANES],
                                   qblk[:, (2 * g + 1) * LANES:(2 * g + 2) * LANES]], axis=0)
            kwin = kvar[2 * g + e, pl.ds(start, 3 * BLOCK), :]
            scores.append(lax.dot_general(lhs, kwin, (((1,), (1,)), ((), ())),
                                          preferred_element_type=F32))
        probs, sink_mass = [], []
        for (g, e), s in zip(units, scores):
            u = 2 * g + e
            s_prev = jnp.where(lane_minus_t >= prev_min, s[:, :BLOCK] + bias_ref[u, :, :BLOCK], NEG_INF)
            s_own = s[:, BLOCK:2 * BLOCK] + bias_ref[u, :, BLOCK:2 * BLOCK]
            s_next = jnp.where(lane_minus_t <= next_max, s[:, 2 * BLOCK:] + bias_ref[u, :, 2 * BLOCK:], NEG_INF)
            sink = jnp.where(first_head, sink_ref[layer, 4 * g + e], sink_ref[layer, 4 * g + 2 + e]) * LOG2E
            row_max = jnp.max(jnp.maximum(jnp.maximum(s_prev, s_own), s_next), axis=-1, keepdims=True)
            m = jnp.maximum(row_max, sink)
            p = jnp.concatenate([jnp.exp2(s_prev - m), jnp.exp2(s_own - m), jnp.exp2(s_next - m)], axis=-1)
            probs.append(p.astype(BF16))
            sink_mass.append(jnp.exp2(sink - m))
        outs = []
        for (g, e), p in zip(units, probs):
            vwin = vvar[2 * g + e, pl.ds(start, 3 * BLOCK), :]
            outs.append(jnp.dot(p, vwin, preferred_element_type=F32))
        chunks = []
        for g in range(N_KV_HEADS):
            for p in range(2):
                rows = slice(p * BLOCK, (p + 1) * BLOCK)
                even, odd = outs[2 * g][rows], outs[2 * g + 1][rows]
                num = jnp.where(lo_out, even, odd)
                den = (pltpu.roll(jnp.where(lo_out, odd, even), HALF, 1)
                       + jnp.where(lo_out, sink_mass[2 * g][rows], sink_mass[2 * g + 1][rows]))
                chunks.append(num / den)
        y = jnp.concatenate(chunks, axis=-1)
        o_ref[pl.ds(start, BLOCK), :] = _rms(y, gn).astype(BF16)
        return carry

    lax.fori_loop(0, nb, block, 0, unroll=4)


def _attn(l, q, k, v, bias, sink, gn, batch, seq):
    assert seq % ATT_SETUP_ROWS == 0

    def seq_rows(width):
        return pl.BlockSpec((seq, width), lambda b: (b, 0))

    return pl.pallas_call(
        functools.partial(_attn_kernel, seq=seq, layer=l),
        grid=(batch,),
        in_specs=[seq_rows(ATT_WIDTH), seq_rows(KV_WIDTH), seq_rows(KV_WIDTH),
                  pl.BlockSpec(bias.shape, lambda b: (0, 0, 0), pipeline_mode=pl.Buffered(1)),
                  pl.BlockSpec(memory_space=pltpu.SMEM),
                  _layer(gn, l)],
        out_specs=seq_rows(ATT_WIDTH),
        out_shape=jax.ShapeDtypeStruct((batch * seq, ATT_WIDTH), BF16),
        scratch_shapes=[pltpu.VMEM((2 * N_KV_HEADS, seq + 2 * BLOCK, LANES), BF16)] * 2,
        compiler_params=pltpu.CompilerParams(dimension_semantics=("arbitrary",),
                                             vmem_limit_bytes=VMEM_LIMIT),
        name="windowed_gqa",
    )(q, k, v, bias, sink, gn)


def _t5_bucket_index():
    t = jnp.arange(BLOCK)[:, None]
    j = jnp.arange(3 * BLOCK)[None, :]
    rel = j - BLOCK - t
    half = N_BUCKETS // 2
    max_exact = half // 2
    ret = (rel > 0).astype(jnp.int32) * half
    n = jnp.abs(rel)
    n_f = jnp.maximum(n, 1).astype(jnp.float32)
    large = max_exact + (jnp.log(n_f / max_exact) / math.log(MAX_DISTANCE / max_exact)
                         * (half - max_exact)).astype(jnp.int32)
    large = jnp.minimum(large, half - 1)
    return ret + jnp.where(n < max_exact, n, large)


def _gate_weights(w_a, b_a, w_x, b_x):
    depth = w_a.shape[0]
    n_slabs = LRU_WIDTH // LANES
    per_slab = LANES // LRU_BLOCK_W

    def slab_blockdiag(w):
        w = w.reshape(depth, n_slabs, per_slab, LRU_BLOCK_W, LRU_BLOCK_W)
        eye = jnp.eye(per_slab, dtype=w.dtype)
        full = jnp.einsum('lspcd,pq->lspcqd', w, eye)
        return full.reshape(depth, n_slabs, LANES, LANES)

    mats = [slab_blockdiag(w[:, d]) for d in range(2) for w in (w_a, w_x)]
    wg = (0.5 * jnp.concatenate(mats, axis=-1)).astype(BF16)
    biases = [b[:, d].reshape(depth, n_slabs, 1, LANES) for d in range(2) for b in (b_a, b_x)]
    bg = 0.5 * jnp.concatenate(biases, axis=-1)
    return wg, bg


def kernel(x, ffn1_norm, ffn1_w_gate, ffn1_w_up, ffn1_w_down, mix_norm, w_in, conv_w, conv_b, lru_w_a, lru_b_a, lru_w_x, lru_b_x, lru_lambda, attn_sink, rel_bias, lru_out_norm, attn_out_norm, w_out, ffn2_norm, ffn2_w_gate, ffn2_w_up, ffn2_w_down, final_norm):
    batch, seq, d_model = x.shape
    depth = ffn1_norm.shape[0]
    tokens = batch * seq
    assert d_model == D_MODEL and seq % BLOCK == 0 and tokens % FFN1_TILE == 0 and tokens % FFN2_TILE == 0
    xt = x.reshape(tokens, D_MODEL)
    bias = _rel_bias_table(_t5_bucket_index(), rel_bias)
    rows = lambda p: p.reshape(depth, 1, -1)
    gate_w, gate_b = _gate_weights(lru_w_a, lru_b_a, lru_w_x, lru_b_x)
    n1, nm, n2 = rows(ffn1_norm), rows(mix_norm), rows(ffn2_norm)
    nr, na, cb = rows(lru_out_norm), rows(attn_out_norm), rows(conv_b)
    nf = final_norm.reshape(1, -1)
    ffn1_stacks = (ffn1_w_gate, ffn1_w_up, ffn1_w_down, w_in)
    ffn2_stacks = (ffn2_w_gate, ffn2_w_up, ffn2_w_down, w_out)
    ffn1_w = tuple(w[0].astype(BF16) for w in ffn1_stacks)
    for l in range(depth):
        (xt, xr, gate, q, k, v), ffn2_w = _ffn1_inproj(l, xt, n1, *ffn1_w[:3], nm, ffn1_w[3], ffn2_stacks)
        last = l == depth - 1
        y_rec, ffn1_w = _rec(l, xr, gate, conv_w, cb, gate_w, gate_b, lru_lambda, batch, seq,
                             () if last else ffn1_stacks, l + 1)
        y_att = _attn(l, q, k, v, bias, attn_sink, na, batch, seq)
        xt = _outproj_ffn2(l, xt, y_rec, y_att, nr, ffn2_w[3], n2, *ffn2_w[:3], nf, final_norm=last)
    return xt.reshape(batch, seq, D_MODEL)
```

```python
import functools
import math

import jax
import jax.numpy as jnp
from jax import lax
from jax.experimental import pallas as pl
from jax.experimental.pallas import tpu as pltpu

F32 = jnp.float32
BF16 = jnp.bfloat16

D_MODEL = 1024
LRU_WIDTH = 512
LRU_BLOCK_W = 64
LRU_C = 8.0
CONV_WIDTH = 4
N_HEADS = 8
N_KV_HEADS = 2
KV_GROUP = N_HEADS // N_KV_HEADS
HEAD_DIM = 64
ATT_WIDTH = N_HEADS * HEAD_DIM
KV_WIDTH = N_KV_HEADS * HEAD_DIM
WINDOW = 128
BLOCK = 128
N_BUCKETS = 32
MAX_DISTANCE = 128
D_FF = 2816
FFN_RES = 0.5
EPS = 1e-6
NEG_INF = -1e30
D_IN = 2 * LRU_WIDTH + ATT_WIDTH + 2 * KV_WIDTH
LOG2E = math.log2(math.e)

LANES = 128
SUBLANES = 8
FF_CHUNK = 256
N_FF_CHUNKS = D_FF // FF_CHUNK
FFN1_TILE = 512
FFN2_TILE = 1024
VMEM_LIMIT = 56 * 1024 * 1024

SCAN_SEGS = 32
SCAN_VREGS = SCAN_SEGS // SUBLANES
SEG_PITCH = 129
REC_CHUNK = 512
PAD_ROWS = SUBLANES


def _rms(x, g):
    return x * lax.rsqrt(jnp.mean(x * x, axis=-1, keepdims=True) + EPS) * g


def _sigmoid(x):
    return 0.5 * jnp.tanh(0.5 * x) + 0.5


def _inv_rms(x):
    return lax.rsqrt(jnp.mean(x * x, axis=-1, keepdims=True) + EPS)


def _swiglu_into(x_ref, inv_rms, wg_ref, wu_ref, wd_ref, acc_ref):
    scale = jnp.broadcast_to(inv_rms, (x_ref.shape[0], FF_CHUNK))
    for c in range(N_FF_CHUNKS):
        cols = slice(c * FF_CHUNK, (c + 1) * FF_CHUNK)
        g = scale * jnp.dot(x_ref[...], wg_ref[:, cols], preferred_element_type=F32)
        u = scale * jnp.dot(x_ref[...], wu_ref[:, cols], preferred_element_type=F32)
        act = (g * _sigmoid(g) * u).astype(BF16)
        part = jnp.dot(act, wd_ref[cols, :], preferred_element_type=F32)
        if c == 0:
            acc_ref[...] = part
        else:
            acc_ref[...] += part


N_CAST = 4


def _cast_blocks(src_refs, dst_refs):
    n = len(dst_refs)
    for w, gain, dst in zip(src_refs[:n], src_refs[n:], dst_refs):
        dst[...] = (w[...] * gain[...]).astype(BF16)


def _ffn1_inproj_kernel(x_ref, wg_ref, wu_ref, wd_ref, win_ref, *rest):
    cast_src, rest = rest[:2 * N_CAST], rest[2 * N_CAST:]
    xo_ref, xr_ref, gate_ref, q_ref, k_ref, v_ref = rest[:6]
    cast_dst, (xb_ref, acc_ref) = rest[6:6 + N_CAST], rest[6 + N_CAST:]
    _cast_blocks(cast_src, cast_dst)
    x = x_ref[...]
    xb_ref[...] = x.astype(BF16)
    _swiglu_into(xb_ref, _inv_rms(x), wg_ref, wu_ref, wd_ref, acc_ref)
    x1 = x + acc_ref[...]
    xo_ref[...] = x1
    xb_ref[...] = x1.astype(BF16)
    inv1 = _inv_rms(x1)

    def proj(rows, lo, width):
        return inv1[rows] * jnp.dot(xb_ref[rows, :], win_ref[:, lo:lo + width], preferred_element_type=F32)

    every = slice(None)
    tile = 2 * LANES
    for lo in range(0, LRU_WIDTH, tile):
        gate_ref[:, lo:lo + tile] = jax.nn.gelu(proj(every, LRU_WIDTH + lo, tile))
        xr_ref[:, lo:lo + tile] = proj(every, lo, tile)
    o = 2 * LRU_WIDTH
    q_ref[...] = (proj(every, o, ATT_WIDTH) * (HEAD_DIM ** -0.5 * LOG2E)).astype(BF16)
    o += ATT_WIDTH
    half = x_ref.shape[0] // 2
    for rows in (slice(0, half), slice(half, 2 * half)):
        kv = proj(rows, o, 2 * KV_WIDTH)
        k_ref[rows, :] = kv[:, :KV_WIDTH].astype(BF16)
        v_ref[rows, :] = kv[:, KV_WIDTH:].astype(BF16)


def _outproj_ffn2_kernel(x_ref, yr_ref, ya_ref, wo_ref, wg_ref, wu_ref, wd_ref, nf_ref,
                         xo_ref, xb_ref, acc_ref, *, final_norm):
    yr = yr_ref[...]
    x1 = (x_ref[...]
          + _inv_rms(yr) * jnp.dot(yr.astype(BF16), wo_ref[:LRU_WIDTH, :], preferred_element_type=F32)
          + jnp.dot(ya_ref[...], wo_ref[LRU_WIDTH:, :], preferred_element_type=F32))
    xb_ref[...] = x1.astype(BF16)
    _swiglu_into(xb_ref, _inv_rms(x1), wg_ref, wu_ref, wd_ref, acc_ref)
    x2 = x1 + acc_ref[...]
    if final_norm:
        x2 = _rms(x2, nf_ref[...])
    xo_ref[...] = x2


def _whole(arr):
    assert arr.ndim == 2
    return pl.BlockSpec(arr.shape, lambda *_: (0, 0), pipeline_mode=pl.Buffered(1))


def _rows(width, tm):
    return pl.BlockSpec((tm, width), lambda i: (i, 0))


def _cast_plan(stacks, gains, l, n_steps, step_of):
    bf16_rows = 2 * SUBLANES
    w_specs, g_specs, out_specs, out_shapes = [], [], [], []
    for w, gain in zip(stacks, gains):
        _, r, c = w.shape
        assert gain.shape == (w.shape[0], r, 1)
        n_blocks = n_steps
        while r % n_blocks or (r // n_blocks) % bf16_rows:
            n_blocks //= 2
        assert n_blocks >= 1 and n_steps % n_blocks == 0
        rep, rows = n_steps // n_blocks, r // n_blocks
        w_specs.append(pl.BlockSpec((None, rows, c), lambda *g, rep=rep: (l, step_of(*g) // rep, 0)))
        g_specs.append(pl.BlockSpec((None, rows, 1), lambda *g, rep=rep: (l, step_of(*g) // rep, 0)))
        out_specs.append(pl.BlockSpec((rows, c), lambda *g, rep=rep: (step_of(*g) // rep, 0)))
        out_shapes.append(jax.ShapeDtypeStruct((r, c), BF16))
    return w_specs + g_specs, out_specs, out_shapes


def _ffn1_inproj(l, x, wg, wu, wd, win, cast_stacks, cast_gains):
    t = x.shape[0]
    tm = FFN1_TILE
    out_shape = [
        jax.ShapeDtypeStruct((t, D_MODEL), F32),
        jax.ShapeDtypeStruct((t, LRU_WIDTH), F32),
        jax.ShapeDtypeStruct((t, LRU_WIDTH), F32),
        jax.ShapeDtypeStruct((t, ATT_WIDTH), BF16),
        jax.ShapeDtypeStruct((t, KV_WIDTH), BF16),
        jax.ShapeDtypeStruct((t, KV_WIDTH), BF16),
    ]
    out_specs = [_rows(D_MODEL, tm), _rows(LRU_WIDTH, tm), _rows(LRU_WIDTH, tm), _rows(ATT_WIDTH, tm),
                 _rows(KV_WIDTH, tm), _rows(KV_WIDTH, tm)]
    cast_in, cast_out, cast_shapes = _cast_plan(cast_stacks, cast_gains, l, t // tm, lambda i: i)
    outs = pl.pallas_call(
        _ffn1_inproj_kernel,
        grid=(t // tm,),
        in_specs=[_rows(D_MODEL, tm), _whole(wg), _whole(wu), _whole(wd), _whole(win)] + cast_in,
        out_specs=out_specs + cast_out,
        out_shape=out_shape + cast_shapes,
        scratch_shapes=[pltpu.VMEM((tm, D_MODEL), BF16), pltpu.VMEM((tm, D_MODEL), F32)],
        compiler_params=pltpu.CompilerParams(dimension_semantics=("arbitrary",),
                                             vmem_limit_bytes=VMEM_LIMIT),
        name="ffn1_inproj",
    )(x, wg, wu, wd, win, *cast_stacks, *cast_gains)
    return outs[:6], outs[6:]


def _outproj_ffn2(x, yr, ya, wo, wg, wu, wd, nf, final_norm):
    t = x.shape[0]
    tm = FFN2_TILE
    return pl.pallas_call(
        functools.partial(_outproj_ffn2_kernel, final_norm=final_norm),
        grid=(t // tm,),
        in_specs=[_rows(D_MODEL, tm), _rows(LRU_WIDTH, tm), _rows(ATT_WIDTH, tm), _whole(wo),
                  _whole(wg), _whole(wu), _whole(wd), _whole(nf)],
        out_specs=_rows(D_MODEL, tm),
        out_shape=jax.ShapeDtypeStruct((t, D_MODEL), F32),
        scratch_shapes=[pltpu.VMEM((tm, D_MODEL), BF16), pltpu.VMEM((tm, D_MODEL), F32)],
        compiler_params=pltpu.CompilerParams(dimension_semantics=("arbitrary",),
                                             vmem_limit_bytes=VMEM_LIMIT),
        name="outproj_ffn2",
    )(x, yr, ya, wo, wg, wu, wd, nf)


def _rec_kernel(xr_ref, gate_ref, cw_ref, cb_ref, wg_ref, bg_ref, lam_ref, *rest, seq, n_cast):
    cast_src, o_ref = rest[:2 * n_cast], rest[2 * n_cast]
    cast_dst = rest[2 * n_cast + 1:3 * n_cast + 1]
    xpad, a_f, u_f, a_b, u_b, h_f, h_b = rest[3 * n_cast + 1:]
    _cast_blocks(cast_src, cast_dst)
    n_chunks = seq // REC_CHUNK
    n_steps = SEG_PITCH
    pad_rows = SCAN_SEGS * SEG_PITCH - seq

    zeros_pad = jnp.zeros((PAD_ROWS, LANES), F32)
    xpad[0:PAD_ROWS, :] = zeros_pad
    xpad[PAD_ROWS + seq:2 * PAD_ROWS + seq, :] = zeros_pad
    xpad[PAD_ROWS:PAD_ROWS + seq, :] = xr_ref[...]

    for a_ref, u_ref in ((a_f, u_f), (a_b, u_b)):
        a_ref[seq:seq + pad_rows, :] = jnp.ones((pad_rows, LANES), F32)
        u_ref[seq:seq + pad_rows, :] = jnp.zeros((pad_rows, LANES), F32)

    lam = lam_ref[...]
    neg_softplus = -(jnp.maximum(-lam, 0.0) + jnp.log1p(jnp.exp(-jnp.abs(lam))))
    half_log2_a = (0.5 * LRU_C * LOG2E) * neg_softplus

    cw = cw_ref[...]
    cb = cb_ref[...]
    for c in range(n_chunks):
        base = PAD_ROWS + c * REC_CHUNK
        xc = cb
        for tap in range(CONV_WIDTH):
            lo = base + tap - CONV_WIDTH // 2
            xc = xc + cw[tap:tap + 1, :] * xpad[lo:lo + REC_CHUNK, :]
        th = jnp.tanh(jnp.dot(xc.astype(BF16), wg_ref[...], preferred_element_type=F32) + bg_ref[...])
        half_xc = 0.5 * xc
        rows = slice(c * REC_CHUNK, (c + 1) * REC_CHUNK)
        for d, (a_ref, u_ref) in enumerate(((a_f, u_f), (a_b, u_b))):
            th_r = th[:, (2 * d) * LANES:(2 * d + 1) * LANES]
            th_i = th[:, (2 * d + 1) * LANES:(2 * d + 2) * LANES]
            scale = half_log2_a[d:d + 1, :]
            a = jnp.exp2(scale * th_r + scale)
            v = 1.0 - a * a
            root = jnp.where(v > 0.0, v * lax.rsqrt(v), 0.0)
            a_ref[rows, :] = a
            u_ref[rows, :] = root * (th_i * half_xc + half_xc)

    def seg_rows(j):
        return pl.ds(j, SCAN_SEGS, stride=SEG_PITCH)

    zeros = jnp.zeros((SCAN_SEGS, LANES), F32)
    ones = jnp.ones((SCAN_SEGS, LANES), F32)

    def segment_maps(j, carry):
        hf, pf, hb, pb = carry
        jb = n_steps - 1 - j
        af = a_f[seg_rows(j), :]
        ab = a_b[seg_rows(jb), :]
        hf = af * hf + u_f[seg_rows(j), :]
        hb = ab * hb + u_b[seg_rows(jb), :]
        return hf, pf * af, hb, pb * ab

    hf, pf, hb, pb = lax.fori_loop(0, n_steps, segment_maps, (zeros, ones, zeros, ones), unroll=4)

    seg = lax.broadcasted_iota(jnp.int32, (SCAN_SEGS, LANES), 0)

    def entering(h, p, toward_higher):
        d = 1
        while d < SCAN_SEGS:
            shift = d if toward_higher else SCAN_SEGS - d
            have = (seg >= d) if toward_higher else (seg < SCAN_SEGS - d)
            h = jnp.where(have, h + p * pltpu.roll(h, shift, 0), h)
            p = jnp.where(have, p * pltpu.roll(p, shift, 0), p)
            d *= 2
        one_step = 1 if toward_higher else SCAN_SEGS - 1
        edge = (seg == 0) if toward_higher else (seg == SCAN_SEGS - 1)
        return jnp.where(edge, 0.0, pltpu.roll(h, one_step, 0))

    hf0 = entering(hf, pf, True)
    hb0 = entering(hb, pb, False)

    def true_states(j, carry):
        hf, hb = carry
        jb = n_steps - 1 - j
        hf = a_f[seg_rows(j), :] * hf + u_f[seg_rows(j), :]
        hb = a_b[seg_rows(jb), :] * hb + u_b[seg_rows(jb), :]
        h_f[seg_rows(j), :] = hf
        h_b[seg_rows(jb), :] = hb
        return hf, hb

    lax.fori_loop(0, n_steps, true_states, (hf0, hb0), unroll=4)

    for c in range(n_chunks):
        rows = slice(c * REC_CHUNK, (c + 1) * REC_CHUNK)
        o_ref[rows, :] = gate_ref[rows, :] * (h_f[rows, :] + h_b[rows, :])


def _rec(l, xr, gate, cw, cb, wg, bg, lam, batch, seq, cast_stacks, cast_gains, cast_layer):
    n_slabs = LRU_WIDTH // LANES
    assert SCAN_SEGS * SEG_PITCH >= seq > (SCAN_SEGS - 1) * SEG_PITCH and seq % REC_CHUNK == 0
    slab = pl.BlockSpec((seq, LANES), lambda b, s: (b, s))
    scan_rows = SCAN_SEGS * SEG_PITCH
    cast_in, cast_out, cast_shapes = _cast_plan(cast_stacks, cast_gains, cast_layer, batch * n_slabs,
                                                lambda b, s: b * n_slabs + s)
    outs = pl.pallas_call(
        functools.partial(_rec_kernel, seq=seq, n_cast=len(cast_stacks)),
        grid=(batch, n_slabs),
        in_specs=[slab, slab,
                  pl.BlockSpec((None, CONV_WIDTH, LANES), lambda b, s: (l, 0, s)),
                  pl.BlockSpec((None, 1, LANES), lambda b, s: (l, 0, s)),
                  pl.BlockSpec((None, None, LANES, 4 * LANES), lambda b, s: (l, s, 0, 0)),
                  pl.BlockSpec((None, None, 1, 4 * LANES), lambda b, s: (l, s, 0, 0)),
                  pl.BlockSpec((None, 2, LANES), lambda b, s: (l, 0, s))] + cast_in,
        out_specs=[slab] + cast_out,
        out_shape=[jax.ShapeDtypeStruct((batch * seq, LRU_WIDTH), F32)] + cast_shapes,
        scratch_shapes=[pltpu.VMEM((seq + 2 * PAD_ROWS, LANES), F32)]
                       + [pltpu.VMEM((scan_rows, LANES), F32)] * 6,
        compiler_params=pltpu.CompilerParams(dimension_semantics=("arbitrary", "arbitrary"),
                                             vmem_limit_bytes=VMEM_LIMIT),
        name="rg_lru",
    )(xr, gate, cw, cb, wg, bg, lam, *cast_stacks, *cast_gains)
    return outs[0], outs[1:]


def _bias_kernel(buckets_ref, table_ref, o_ref):
    buckets = buckets_ref[...]
    for h in range(N_HEADS):
        acc = jnp.zeros(buckets.shape, F32)
        for b in range(N_BUCKETS):
            acc = jnp.where(buckets == b, table_ref[b, h] * LOG2E, acc)
        g, p, e = h // KV_GROUP, (h % KV_GROUP) // 2, h % 2
        o_ref[2 * g + e, p * BLOCK:(p + 1) * BLOCK, :] = acc


def _rel_bias_table(buckets, rel_bias):
    return pl.pallas_call(
        _bias_kernel,
        in_specs=[pl.BlockSpec(memory_space=pltpu.VMEM), pl.BlockSpec(memory_space=pltpu.SMEM)],
        out_specs=pl.BlockSpec(memory_space=pltpu.VMEM),
        out_shape=jax.ShapeDtypeStruct((N_HEADS // 2, 2 * BLOCK, 3 * BLOCK), F32),
        name="rel_bias",
    )(buckets, rel_bias)


ATT_SETUP_ROWS = 512
HALF = LANES // 2


def _attn_kernel(q_ref, k_ref, v_ref, bias_ref, sink_ref, o_ref, kvar, vvar, *, seq, layer):
    nb = seq // BLOCK
    lo_half = lax.broadcasted_iota(jnp.int32, (ATT_SETUP_ROWS, LANES), 1) < HALF
    zeros_blk = jnp.zeros((BLOCK, LANES), BF16)
    for u in range(2 * N_KV_HEADS):
        for ref in (kvar, vvar):
            ref[u, 0:BLOCK, :] = zeros_blk
            ref[u, BLOCK + seq:2 * BLOCK + seq, :] = zeros_blk
    for r in range(seq // ATT_SETUP_ROWS):
        src = slice(r * ATT_SETUP_ROWS, (r + 1) * ATT_SETUP_ROWS)
        dst = slice(BLOCK + r * ATT_SETUP_ROWS, BLOCK + (r + 1) * ATT_SETUP_ROWS)
        kc = k_ref[src, :].astype(F32)
        vc = v_ref[src, :].astype(F32)
        kr = pltpu.roll(kc, HALF, 1)
        vr = pltpu.roll(vc, HALF, 1)
        kvar[0, dst, :] = jnp.where(lo_half, kc, 0.0).astype(BF16)
        kvar[1, dst, :] = jnp.where(lo_half, 0.0, kr).astype(BF16)
        kvar[2, dst, :] = jnp.where(lo_half, kr, 0.0).astype(BF16)
        kvar[3, dst, :] = jnp.where(lo_half, 0.0, kc).astype(BF16)
        vvar[0, dst, :] = jnp.where(lo_half, vc, 1.0).astype(BF16)
        vvar[1, dst, :] = jnp.where(lo_half, 1.0, vr).astype(BF16)
        vvar[2, dst, :] = jnp.where(lo_half, vr, 1.0).astype(BF16)
        vvar[3, dst, :] = jnp.where(lo_half, 1.0, vc).astype(BF16)

    assert BLOCK == WINDOW == LANES
    t_idx = lax.broadcasted_iota(jnp.int32, (2 * BLOCK, LANES), 0) % BLOCK
    lane_minus_t = lax.broadcasted_iota(jnp.int32, (2 * BLOCK, LANES), 1) - t_idx
    first_head = lax.broadcasted_iota(jnp.int32, (2 * BLOCK, 1), 0) < BLOCK
    lo_out = lax.broadcasted_iota(jnp.int32, (BLOCK, LANES), 1) < HALF
    units = [(g, e) for g in range(N_KV_HEADS) for e in range(2)]
    never = 2 * LANES

    def block(n, carry):
        start = pl.multiple_of(n * BLOCK, BLOCK)
        prev_min = jnp.where(n == 0, never, 0)
        next_max = jnp.where(n == nb - 1, -never, 0)
        qblk = q_ref[pl.ds(start, BLOCK), :]
        scores = []
        for g, e in units:
            lhs = jnp.concatenate([qblk[:, (2 * g) * LANES:(2 * g + 1) * LANES],
                                   qblk[:, (2 * g + 1) * LANES:(2 * g + 2) * LANES]], axis=0)
            kwin = kvar[2 * g + e, pl.ds(start, 3 * BLOCK), :]
            scores.append(lax.dot_general(lhs, kwin, (((1,), (1,)), ((), ())),
                                          preferred_element_type=F32))
        probs, sink_mass = [], []
        for (g, e), s in zip(units, scores):
            u = 2 * g + e
            s_prev = jnp.where(lane_minus_t >= prev_min, s[:, :BLOCK] + bias_ref[u, :, :BLOCK], NEG_INF)
            s_own = s[:, BLOCK:2 * BLOCK] + bias_ref[u, :, BLOCK:2 * BLOCK]
            s_next = jnp.where(lane_minus_t <= next_max, s[:, 2 * BLOCK:] + bias_ref[u, :, 2 * BLOCK:], NEG_INF)
            sink = jnp.where(first_head, sink_ref[layer, 4 * g + e], sink_ref[layer, 4 * g + 2 + e]) * LOG2E
            row_max = jnp.max(jnp.maximum(jnp.maximum(s_prev, s_own), s_next), axis=-1, keepdims=True)
            m = jnp.maximum(row_max, sink)
            p = jnp.concatenate([jnp.exp2(s_prev - m), jnp.exp2(s_own - m), jnp.exp2(s_next - m)], axis=-1)
            probs.append(p.astype(BF16))
            sink_mass.append(jnp.exp2(sink - m))
        outs = []
        for (g, e), p in zip(units, probs):
            vwin = vvar[2 * g + e, pl.ds(start, 3 * BLOCK), :]
            outs.append(jnp.dot(p, vwin, preferred_element_type=F32))
        chunks = []
        for g in range(N_KV_HEADS):
            for p in range(2):
                rows = slice(p * BLOCK, (p + 1) * BLOCK)
                even, odd = outs[2 * g][rows], outs[2 * g + 1][rows]
                num = jnp.where(lo_out, even, odd)
                den = (pltpu.roll(jnp.where(lo_out, odd, even), HALF, 1)
                       + jnp.where(lo_out, sink_mass[2 * g][rows], sink_mass[2 * g + 1][rows]))
                chunks.append(num / den)
        y = jnp.concatenate(chunks, axis=-1)
        o_ref[pl.ds(start, BLOCK), :] = (y * _inv_rms(y)).astype(BF16)
        return carry

    lax.fori_loop(0, nb, block, 0, unroll=4)


def _attn(l, q, k, v, bias, sink, batch, seq):
    assert seq % ATT_SETUP_ROWS == 0

    def seq_rows(width):
        return pl.BlockSpec((seq, width), lambda b: (b, 0))

    return pl.pallas_call(
        functools.partial(_attn_kernel, seq=seq, layer=l),
        grid=(batch,),
        in_specs=[seq_rows(ATT_WIDTH), seq_rows(KV_WIDTH), seq_rows(KV_WIDTH),
                  pl.BlockSpec(bias.shape, lambda b: (0, 0, 0), pipeline_mode=pl.Buffered(1)),
                  pl.BlockSpec(memory_space=pltpu.SMEM)],
        out_specs=seq_rows(ATT_WIDTH),
        out_shape=jax.ShapeDtypeStruct((batch * seq, ATT_WIDTH), BF16),
        scratch_shapes=[pltpu.VMEM((2 * N_KV_HEADS, seq + 2 * BLOCK, LANES), BF16)] * 2,
        compiler_params=pltpu.CompilerParams(dimension_semantics=("arbitrary",),
                                             vmem_limit_bytes=VMEM_LIMIT),
        name="windowed_gqa",
    )(q, k, v, bias, sink)


def _t5_bucket_index():
    t = jnp.arange(BLOCK)[:, None]
    j = jnp.arange(3 * BLOCK)[None, :]
    rel = j - BLOCK - t
    half = N_BUCKETS // 2
    max_exact = half // 2
    ret = (rel > 0).astype(jnp.int32) * half
    n = jnp.abs(rel)
    n_f = jnp.maximum(n, 1).astype(jnp.float32)
    large = max_exact + (jnp.log(n_f / max_exact) / math.log(MAX_DISTANCE / max_exact)
                         * (half - max_exact)).astype(jnp.int32)
    large = jnp.minimum(large, half - 1)
    return ret + jnp.where(n < max_exact, n, large)


def _gate_weights(w_a, b_a, w_x, b_x):
    depth = w_a.shape[0]
    n_slabs = LRU_WIDTH // LANES
    per_slab = LANES // LRU_BLOCK_W

    def slab_blockdiag(w):
        w = w.reshape(depth, n_slabs, per_slab, LRU_BLOCK_W, LRU_BLOCK_W)
        eye = jnp.eye(per_slab, dtype=w.dtype)
        full = jnp.einsum('lspcd,pq->lspcqd', w, eye)
        return full.reshape(depth, n_slabs, LANES, LANES)

    mats = [slab_blockdiag(w[:, d]) for d in range(2) for w in (w_a, w_x)]
    wg = (0.5 * jnp.concatenate(mats, axis=-1)).astype(BF16)
    biases = [b[:, d].reshape(depth, n_slabs, 1, LANES) for d in range(2) for b in (b_a, b_x)]
    bg = 0.5 * jnp.concatenate(biases, axis=-1)
    return wg, bg


def kernel(x, ffn1_norm, ffn1_w_gate, ffn1_w_up, ffn1_w_down, mix_norm, w_in, conv_w, conv_b, lru_w_a, lru_b_a, lru_w_x, lru_b_x, lru_lambda, attn_sink, rel_bias, lru_out_norm, attn_out_norm, w_out, ffn2_norm, ffn2_w_gate, ffn2_w_up, ffn2_w_down, final_norm):
    batch, seq, d_model = x.shape
    depth = ffn1_norm.shape[0]
    tokens = batch * seq
    assert d_model == D_MODEL and seq % BLOCK == 0 and tokens % FFN1_TILE == 0 and tokens % FFN2_TILE == 0
    xt = x.reshape(tokens, D_MODEL)
    bias = _rel_bias_table(_t5_bucket_index(), rel_bias)
    gate_w, gate_b = _gate_weights(lru_w_a, lru_b_a, lru_w_x, lru_b_x)
    cb = conv_b.reshape(depth, 1, -1)
    nf = final_norm.reshape(1, -1)
    col = lambda g: g[:, :, None]
    half_step = jnp.full((depth, D_FF, 1), FFN_RES, F32)
    ffn1_stacks = (ffn1_w_gate, ffn1_w_up, ffn1_w_down, w_in)
    ffn1_gains = (col(ffn1_norm), col(ffn1_norm), half_step, col(mix_norm))
    ffn2_stacks = (ffn2_w_gate, ffn2_w_up, ffn2_w_down, w_out)
    ffn2_gains = (col(ffn2_norm), col(ffn2_norm), half_step,
                  col(jnp.concatenate([lru_out_norm, attn_out_norm], axis=-1)))
    ffn1_w = tuple((w[0] * g[0]).astype(BF16) for w, g in zip(ffn1_stacks, ffn1_gains))
    for l in range(depth):
        (xt, xr, gate, q, k, v), ffn2_w = _ffn1_inproj(l, xt, *ffn1_w, ffn2_stacks, ffn2_gains)
        last = l == depth - 1
        y_rec, ffn1_w = _rec(l, xr, gate, conv_w, cb, gate_w, gate_b, lru_lambda, batch, seq,
                             () if last else ffn1_stacks, () if last else ffn1_gains, l + 1)
        y_att = _attn(l, q, k, v, bias, attn_sink, batch, seq)
        xt = _outproj_ffn2(xt, y_rec, y_att, ffn2_w[3], *ffn2_w[:3], nf, final_norm=last)
    return xt.reshape(batch, seq, D_MODEL)
```

```python
import functools
import math

import jax
import jax.numpy as jnp
from jax import lax
from jax.experimental import pallas as pl
from jax.experimental.pallas import tpu as pltpu

F32 = jnp.float32
BF16 = jnp.bfloat16

D_MODEL = 1024
LRU_WIDTH = 512
LRU_BLOCK_W = 64
LRU_C = 8.0
CONV_WIDTH = 4
N_HEADS = 8
N_KV_HEADS = 2
KV_GROUP = N_HEADS // N_KV_HEADS
HEAD_DIM = 64
ATT_WIDTH = N_HEADS * HEAD_DIM
KV_WIDTH = N_KV_HEADS * HEAD_DIM
WINDOW = 128
BLOCK = 128
N_BUCKETS = 32
MAX_DISTANCE = 128
D_FF = 2816
FFN_RES = 0.5
EPS = 1e-6
NEG_INF = -1e30
D_IN = 2 * LRU_WIDTH + ATT_WIDTH + 2 * KV_WIDTH
LOG2E = math.log2(math.e)

LANES = 128
SUBLANES = 8
FF_CHUNK = 256
N_FF_CHUNKS = D_FF // FF_CHUNK
FFN1_TILE = 512
FFN2_TILE = 1024
VMEM_LIMIT = 56 * 1024 * 1024

SCAN_SEGS = 32
SCAN_VREGS = SCAN_SEGS // SUBLANES
SEG_PITCH = 129
REC_CHUNK = 512
GATE_BIAS_ROWS = 2


def _rms(x, g):
    return x * lax.rsqrt(jnp.mean(x * x, axis=-1, keepdims=True) + EPS) * g


def _sigmoid(x):
    return 0.5 * jnp.tanh(0.5 * x) + 0.5


def _inv_rms(x):
    return lax.rsqrt(jnp.mean(x * x, axis=-1, keepdims=True) + EPS)


def _swiglu_into(x_ref, inv_rms, wg_ref, wu_ref, wd_ref, acc_ref):
    scale = None if inv_rms is None else jnp.broadcast_to(inv_rms, (x_ref.shape[0], FF_CHUNK))
    for c in range(N_FF_CHUNKS):
        cols = slice(c * FF_CHUNK, (c + 1) * FF_CHUNK)
        g = jnp.dot(x_ref[...], wg_ref[:, cols], preferred_element_type=F32)
        u = jnp.dot(x_ref[...], wu_ref[:, cols], preferred_element_type=F32)
        if scale is not None:
            g, u = scale * g, scale * u
        act = (g * _sigmoid(g) * u).astype(BF16)
        part = jnp.dot(act, wd_ref[cols, :], preferred_element_type=F32)
        if c == 0:
            acc_ref[...] = part
        else:
            acc_ref[...] += part


N_CAST = 4


def _cast_blocks(src_refs, dst_refs):
    n = len(dst_refs)
    for w, gain, dst in zip(src_refs[:n], src_refs[n:], dst_refs):
        dst[...] = (w[...] * gain[...]).astype(BF16)


def _ffn1_inproj_kernel(x_ref, wg_ref, wu_ref, wd_ref, win_ref, *rest):
    cast_src, rest = rest[:2 * N_CAST], rest[2 * N_CAST:]
    xo_ref, xr_ref, gate_ref, q_ref, k_ref, v_ref = rest[:6]
    cast_dst, (xb_ref, acc_ref) = rest[6:6 + N_CAST], rest[6 + N_CAST:]
    _cast_blocks(cast_src, cast_dst)
    x = x_ref[...]
    xb_ref[...] = x.astype(BF16)
    _swiglu_into(xb_ref, _inv_rms(x), wg_ref, wu_ref, wd_ref, acc_ref)
    x1 = x + acc_ref[...]
    xo_ref[...] = x1
    xb_ref[...] = x1.astype(BF16)
    inv1 = _inv_rms(x1)

    def proj(rows, lo, width):
        return inv1[rows] * jnp.dot(xb_ref[rows, :], win_ref[:, lo:lo + width], preferred_element_type=F32)

    every = slice(None)
    tile = 2 * LANES
    for lo in range(0, LRU_WIDTH, tile):
        gate_ref[:, lo:lo + tile] = jax.nn.gelu(proj(every, LRU_WIDTH + lo, tile))
        xr_ref[:, lo:lo + tile] = proj(every, lo, tile)
    o = 2 * LRU_WIDTH
    q_ref[...] = (proj(every, o, ATT_WIDTH) * (HEAD_DIM ** -0.5 * LOG2E)).astype(BF16)
    o += ATT_WIDTH
    half = x_ref.shape[0] // 2
    for rows in (slice(0, half), slice(half, 2 * half)):
        kv = proj(rows, o, 2 * KV_WIDTH)
        k_ref[rows, :] = kv[:, :KV_WIDTH].astype(BF16)
        v_ref[rows, :] = kv[:, KV_WIDTH:].astype(BF16)


def _outproj_ffn2_kernel(x_ref, yr_ref, ya_ref, wo_ref, wg_ref, wu_ref, wd_ref, nf_ref,
                         xo_ref, xb_ref, acc_ref, *, final_norm):
    yr = yr_ref[...]
    x1 = (x_ref[...]
          + _inv_rms(yr) * jnp.dot(yr.astype(BF16), wo_ref[:LRU_WIDTH, :], preferred_element_type=F32)
          + jnp.dot(ya_ref[...], wo_ref[LRU_WIDTH:, :], preferred_element_type=F32))
    xb_ref[...] = (x1 * _inv_rms(x1)).astype(BF16)
    _swiglu_into(xb_ref, None, wg_ref, wu_ref, wd_ref, acc_ref)
    x2 = x1 + acc_ref[...]
    if final_norm:
        x2 = _rms(x2, nf_ref[...])
    xo_ref[...] = x2


def _whole(arr):
    assert arr.ndim == 2
    return pl.BlockSpec(arr.shape, lambda *_: (0, 0), pipeline_mode=pl.Buffered(1))


def _rows(width, tm):
    return pl.BlockSpec((tm, width), lambda i: (i, 0))


def _cast_plan(stacks, gains, l, n_steps, step_of):
    bf16_rows = 2 * SUBLANES
    w_specs, g_specs, out_specs, out_shapes = [], [], [], []
    for w, gain in zip(stacks, gains):
        _, r, c = w.shape
        assert gain.shape == (w.shape[0], r, 1)
        n_blocks = n_steps
        while r % n_blocks or (r // n_blocks) % bf16_rows:
            n_blocks //= 2
        assert n_blocks >= 1 and n_steps % n_blocks == 0
        rep, rows = n_steps // n_blocks, r // n_blocks
        w_specs.append(pl.BlockSpec((None, rows, c), lambda *g, rep=rep: (l, step_of(*g) // rep, 0)))
        g_specs.append(pl.BlockSpec((None, rows, 1), lambda *g, rep=rep: (l, step_of(*g) // rep, 0)))
        out_specs.append(pl.BlockSpec((rows, c), lambda *g, rep=rep: (step_of(*g) // rep, 0)))
        out_shapes.append(jax.ShapeDtypeStruct((r, c), BF16))
    return w_specs + g_specs, out_specs, out_shapes


def _ffn1_inproj(l, x, wg, wu, wd, win, cast_stacks, cast_gains):
    t = x.shape[0]
    tm = FFN1_TILE
    out_shape = [
        jax.ShapeDtypeStruct((t, D_MODEL), F32),
        jax.ShapeDtypeStruct((t, LRU_WIDTH), F32),
        jax.ShapeDtypeStruct((t, LRU_WIDTH), F32),
        jax.ShapeDtypeStruct((t, ATT_WIDTH), BF16),
        jax.ShapeDtypeStruct((t, KV_WIDTH), BF16),
        jax.ShapeDtypeStruct((t, KV_WIDTH), BF16),
    ]
    out_specs = [_rows(D_MODEL, tm), _rows(LRU_WIDTH, tm), _rows(LRU_WIDTH, tm), _rows(ATT_WIDTH, tm),
                 _rows(KV_WIDTH, tm), _rows(KV_WIDTH, tm)]
    cast_in, cast_out, cast_shapes = _cast_plan(cast_stacks, cast_gains, l, t // tm, lambda i: i)
    outs = pl.pallas_call(
        _ffn1_inproj_kernel,
        grid=(t // tm,),
        in_specs=[_rows(D_MODEL, tm), _whole(wg), _whole(wu), _whole(wd), _whole(win)] + cast_in,
        out_specs=out_specs + cast_out,
        out_shape=out_shape + cast_shapes,
        scratch_shapes=[pltpu.VMEM((tm, D_MODEL), BF16), pltpu.VMEM((tm, D_MODEL), F32)],
        compiler_params=pltpu.CompilerParams(dimension_semantics=("arbitrary",),
                                             vmem_limit_bytes=VMEM_LIMIT),
        name="ffn1_inproj",
    )(x, wg, wu, wd, win, *cast_stacks, *cast_gains)
    return outs[:6], outs[6:]


def _outproj_ffn2(x, yr, ya, wo, wg, wu, wd, nf, final_norm):
    t = x.shape[0]
    tm = FFN2_TILE
    return pl.pallas_call(
        functools.partial(_outproj_ffn2_kernel, final_norm=final_norm),
        grid=(t // tm,),
        in_specs=[_rows(D_MODEL, tm), _rows(LRU_WIDTH, tm), _rows(ATT_WIDTH, tm), _whole(wo),
                  _whole(wg), _whole(wu), _whole(wd), _whole(nf)],
        out_specs=_rows(D_MODEL, tm),
        out_shape=jax.ShapeDtypeStruct((t, D_MODEL), F32),
        scratch_shapes=[pltpu.VMEM((tm, D_MODEL), BF16), pltpu.VMEM((tm, D_MODEL), F32)],
        compiler_params=pltpu.CompilerParams(dimension_semantics=("arbitrary",),
                                             vmem_limit_bytes=VMEM_LIMIT),
        name="outproj_ffn2",
    )(x, yr, ya, wo, wg, wu, wd, nf)


def _rec_kernel(xr_ref, gate_ref, cw_ref, cb_ref, wg_ref, lam_ref, *rest, seq, n_cast):
    cast_src, o_ref = rest[:2 * n_cast], rest[2 * n_cast]
    cast_dst = rest[2 * n_cast + 1:3 * n_cast + 1]
    a_f, u_f, a_b, u_b, h_f, h_b = rest[3 * n_cast + 1:]
    _cast_blocks(cast_src, cast_dst)
    n_chunks = seq // REC_CHUNK
    n_steps = SEG_PITCH
    pad_rows = SCAN_SEGS * SEG_PITCH - seq
    chunk_row = lax.broadcasted_iota(jnp.int32, (REC_CHUNK, LANES), 0)

    def shifted_rows(c, offset):
        lo = c * REC_CHUNK + offset
        if lo >= 0 and lo + REC_CHUNK <= seq:
            return xr_ref[lo:lo + REC_CHUNK, :]
        rolled = pltpu.roll(xr_ref[c * REC_CHUNK:(c + 1) * REC_CHUNK, :], (-offset) % REC_CHUNK, 0)
        outside = (chunk_row < -offset) if offset < 0 else (chunk_row >= REC_CHUNK - offset)
        return jnp.where(outside, 0.0, rolled)

    for a_ref, u_ref in ((a_f, u_f), (a_b, u_b)):
        a_ref[seq:seq + pad_rows, :] = jnp.ones((pad_rows, LANES), F32)
        u_ref[seq:seq + pad_rows, :] = jnp.zeros((pad_rows, LANES), F32)

    lam = lam_ref[...]
    neg_softplus = -(jnp.maximum(-lam, 0.0) + jnp.log1p(jnp.exp(-jnp.abs(lam))))
    half_log2_a = (0.5 * LRU_C * LOG2E) * neg_softplus

    cw = cw_ref[...]
    cb = cb_ref[...]
    bias_cols = (lax.broadcasted_iota(jnp.int32, (REC_CHUNK, LANES), 1) < GATE_BIAS_ROWS).astype(BF16)
    for c in range(n_chunks):
        xc = cb
        for tap in range(CONV_WIDTH):
            xc = xc + cw[tap:tap + 1, :] * shifted_rows(c, tap - CONV_WIDTH // 2)
        lhs = jnp.concatenate([xc.astype(BF16), bias_cols], axis=1)
        th = jnp.tanh(jnp.dot(lhs, wg_ref[...], preferred_element_type=F32))
        half_xc = 0.5 * xc
        rows = slice(c * REC_CHUNK, (c + 1) * REC_CHUNK)
        for d, (a_ref, u_ref) in enumerate(((a_f, u_f), (a_b, u_b))):
            th_r = th[:, (2 * d) * LANES:(2 * d + 1) * LANES]
            th_i = th[:, (2 * d + 1) * LANES:(2 * d + 2) * LANES]
            scale = half_log2_a[d:d + 1, :]
            a = jnp.exp2(scale * th_r + scale)
            v = 1.0 - a * a
            root = jnp.where(v > 0.0, v * lax.rsqrt(v), 0.0)
            a_ref[rows, :] = a
            u_ref[rows, :] = root * (th_i * half_xc + half_xc)

    def seg_rows(j):
        return pl.ds(j, SCAN_SEGS, stride=SEG_PITCH)

    zeros = jnp.zeros((SCAN_SEGS, LANES), F32)
    ones = jnp.ones((SCAN_SEGS, LANES), F32)

    def segment_maps(j, carry):
        hf, pf, hb, pb = carry
        jb = n_steps - 1 - j
        af = a_f[seg_rows(j), :]
        ab = a_b[seg_rows(jb), :]
        hf = af * hf + u_f[seg_rows(j), :]
        hb = ab * hb + u_b[seg_rows(jb), :]
        return hf, pf * af, hb, pb * ab

    hf, pf, hb, pb = lax.fori_loop(0, n_steps, segment_maps, (zeros, ones, zeros, ones), unroll=4)

    seg = lax.broadcasted_iota(jnp.int32, (SCAN_SEGS, LANES), 0)

    def entering(h, p, toward_higher):
        d = 1
        while d < SCAN_SEGS:
            shift = d if toward_higher else SCAN_SEGS - d
            have = (seg >= d) if toward_higher else (seg < SCAN_SEGS - d)
            h = jnp.where(have, h + p * pltpu.roll(h, shift, 0), h)
            p = jnp.where(have, p * pltpu.roll(p, shift, 0), p)
            d *= 2
        one_step = 1 if toward_higher else SCAN_SEGS - 1
        edge = (seg == 0) if toward_higher else (seg == SCAN_SEGS - 1)
        return jnp.where(edge, 0.0, pltpu.roll(h, one_step, 0))

    hf0 = entering(hf, pf, True)
    hb0 = entering(hb, pb, False)

    def true_states(j, carry):
        hf, hb = carry
        jb = n_steps - 1 - j
        hf = a_f[seg_rows(j), :] * hf + u_f[seg_rows(j), :]
        hb = a_b[seg_rows(jb), :] * hb + u_b[seg_rows(jb), :]
        h_f[seg_rows(j), :] = hf
        h_b[seg_rows(jb), :] = hb
        return hf, hb

    lax.fori_loop(0, n_steps, true_states, (hf0, hb0), unroll=4)

    for c in range(n_chunks):
        rows = slice(c * REC_CHUNK, (c + 1) * REC_CHUNK)
        o_ref[rows, :] = gate_ref[rows, :] * (h_f[rows, :] + h_b[rows, :])


def _rec(l, xr, gate, cw, cb, wg, lam, batch, seq, cast_stacks, cast_gains, cast_layer):
    n_slabs = LRU_WIDTH // LANES
    assert SCAN_SEGS * SEG_PITCH >= seq > (SCAN_SEGS - 1) * SEG_PITCH and seq % REC_CHUNK == 0
    slab = pl.BlockSpec((seq, LANES), lambda b, s: (b, s))
    scan_rows = SCAN_SEGS * SEG_PITCH
    cast_in, cast_out, cast_shapes = _cast_plan(cast_stacks, cast_gains, cast_layer, batch * n_slabs,
                                                lambda b, s: b * n_slabs + s)
    outs = pl.pallas_call(
        functools.partial(_rec_kernel, seq=seq, n_cast=len(cast_stacks)),
        grid=(batch, n_slabs),
        in_specs=[slab, slab,
                  pl.BlockSpec((None, CONV_WIDTH, LANES), lambda b, s: (l, 0, s)),
                  pl.BlockSpec((None, 1, LANES), lambda b, s: (l, 0, s)),
                  pl.BlockSpec((None, None, 2 * LANES, 4 * LANES), lambda b, s: (l, s, 0, 0)),
                  pl.BlockSpec((None, 2, LANES), lambda b, s: (l, 0, s))] + cast_in,
        out_specs=[slab] + cast_out,
        out_shape=[jax.ShapeDtypeStruct((batch * seq, LRU_WIDTH), F32)] + cast_shapes,
        scratch_shapes=[pltpu.VMEM((scan_rows, LANES), F32)] * 6,
        compiler_params=pltpu.CompilerParams(dimension_semantics=("arbitrary", "arbitrary"),
                                             vmem_limit_bytes=VMEM_LIMIT),
        name="rg_lru",
    )(xr, gate, cw, cb, wg, lam, *cast_stacks, *cast_gains)
    return outs[0], outs[1:]


def _bias_kernel(buckets_ref, table_ref, o_ref):
    buckets = buckets_ref[...]
    for h in range(N_HEADS):
        acc = jnp.zeros(buckets.shape, F32)
        for b in range(N_BUCKETS):
            acc = jnp.where(buckets == b, table_ref[b, h] * LOG2E, acc)
        g, p, e = h // KV_GROUP, (h % KV_GROUP) // 2, h % 2
        o_ref[2 * g + e, p * BLOCK:(p + 1) * BLOCK, :] = acc


def _rel_bias_table(buckets, rel_bias):
    return pl.pallas_call(
        _bias_kernel,
        in_specs=[pl.BlockSpec(memory_space=pltpu.VMEM), pl.BlockSpec(memory_space=pltpu.SMEM)],
        out_specs=pl.BlockSpec(memory_space=pltpu.VMEM),
        out_shape=jax.ShapeDtypeStruct((N_HEADS // 2, 2 * BLOCK, 3 * BLOCK), F32),
        name="rel_bias",
    )(buckets, rel_bias)


ATT_SETUP_ROWS = 512
ATT_BLOCKS = 1
ATT_UNROLL = 4
HALF = LANES // 2


def _attn_kernel(q_ref, k_ref, v_ref, bias_ref, sink_ref, o_ref, kvar, vvar, *, seq, layer):
    nb = seq // BLOCK
    lo_half = lax.broadcasted_iota(jnp.int32, (ATT_SETUP_ROWS, LANES), 1) < HALF
    zeros_blk = jnp.zeros((BLOCK, LANES), BF16)
    for u in range(2 * N_KV_HEADS):
        for ref in (kvar, vvar):
            ref[u, 0:BLOCK, :] = zeros_blk
            ref[u, BLOCK + seq:2 * BLOCK + seq, :] = zeros_blk
    for r in range(seq // ATT_SETUP_ROWS):
        src = slice(r * ATT_SETUP_ROWS, (r + 1) * ATT_SETUP_ROWS)
        dst = slice(BLOCK + r * ATT_SETUP_ROWS, BLOCK + (r + 1) * ATT_SETUP_ROWS)
        kc = k_ref[src, :]
        vc = v_ref[src, :]
        kr = pltpu.roll(kc, HALF, 1)
        vr = pltpu.roll(vc, HALF, 1)
        zero = jnp.zeros_like(kc)
        one = jnp.ones_like(vc)
        kvar[0, dst, :] = jnp.where(lo_half, kc, zero)
        kvar[1, dst, :] = jnp.where(lo_half, zero, kr)
        kvar[2, dst, :] = jnp.where(lo_half, kr, zero)
        kvar[3, dst, :] = jnp.where(lo_half, zero, kc)
        vvar[0, dst, :] = jnp.where(lo_half, vc, one)
        vvar[1, dst, :] = jnp.where(lo_half, one, vr)
        vvar[2, dst, :] = jnp.where(lo_half, vr, one)
        vvar[3, dst, :] = jnp.where(lo_half, one, vc)

    assert BLOCK == WINDOW == LANES
    t_idx = lax.broadcasted_iota(jnp.int32, (2 * BLOCK, LANES), 0) % BLOCK
    lane_minus_t = lax.broadcasted_iota(jnp.int32, (2 * BLOCK, LANES), 1) - t_idx
    first_head = lax.broadcasted_iota(jnp.int32, (2 * BLOCK, 1), 0) < BLOCK
    lo_out = lax.broadcasted_iota(jnp.int32, (BLOCK, LANES), 1) < HALF
    units = [(g, e) for g in range(N_KV_HEADS) for e in range(2)]
    never = 2 * LANES

    def blocks(i, carry):
        ns = [i * ATT_BLOCKS + j for j in range(ATT_BLOCKS)]
        starts = [pl.multiple_of(n * BLOCK, BLOCK) for n in ns]
        work = [(j, g, e) for j in range(ATT_BLOCKS) for g, e in units]
        scores = []
        for j, g, e in work:
            qblk = q_ref[pl.ds(starts[j], BLOCK), (2 * g) * LANES:(2 * g + 2) * LANES]
            lhs = jnp.concatenate([qblk[:, :LANES], qblk[:, LANES:]], axis=0)
            kwin = kvar[2 * g + e, pl.ds(starts[j], 3 * BLOCK), :]
            scores.append(lax.dot_general(lhs, kwin, (((1,), (1,)), ((), ())),
                                          preferred_element_type=F32))
        probs, sink_mass = [], []
        for (j, g, e), s in zip(work, scores):
            u = 2 * g + e
            prev_min = jnp.where(ns[j] == 0, never, 0)
            next_max = jnp.where(ns[j] == nb - 1, -never, 0)
            s_prev = jnp.where(lane_minus_t >= prev_min, s[:, :BLOCK] + bias_ref[u, :, :BLOCK], NEG_INF)
            s_own = s[:, BLOCK:2 * BLOCK] + bias_ref[u, :, BLOCK:2 * BLOCK]
            s_next = jnp.where(lane_minus_t <= next_max, s[:, 2 * BLOCK:] + bias_ref[u, :, 2 * BLOCK:], NEG_INF)
            sink = jnp.where(first_head, sink_ref[layer, 4 * g + e], sink_ref[layer, 4 * g + 2 + e]) * LOG2E
            row_max = jnp.max(jnp.maximum(jnp.maximum(s_prev, s_own), s_next), axis=-1, keepdims=True)
            m = jnp.maximum(row_max, sink)
            p = jnp.concatenate([jnp.exp2(s_prev - m), jnp.exp2(s_own - m), jnp.exp2(s_next - m)], axis=-1)
            probs.append(p.astype(BF16))
            sink_mass.append(jnp.exp2(sink - m))
        outs = []
        for (j, g, e), p in zip(work, probs):
            vwin = vvar[2 * g + e, pl.ds(starts[j], 3 * BLOCK), :]
            outs.append(jnp.dot(p, vwin, preferred_element_type=F32))
        for j in range(ATT_BLOCKS):
            chunks = []
            for g in range(N_KV_HEADS):
                for p in range(2):
                    rows = slice(p * BLOCK, (p + 1) * BLOCK)
                    ue, uo = j * len(units) + 2 * g, j * len(units) + 2 * g + 1
                    even, odd = outs[ue][rows], outs[uo][rows]
                    num = jnp.where(lo_out, even, odd)
                    den = (pltpu.roll(jnp.where(lo_out, odd, even), HALF, 1)
                           + jnp.where(lo_out, sink_mass[ue][rows], sink_mass[uo][rows]))
                    chunks.append(num / den)
            y = jnp.concatenate(chunks, axis=-1)
            o_ref[pl.ds(starts[j], BLOCK), :] = (y * _inv_rms(y)).astype(BF16)
        return carry

    lax.fori_loop(0, nb // ATT_BLOCKS, blocks, 0, unroll=ATT_UNROLL)


def _attn(l, q, k, v, bias, sink, batch, seq):
    assert seq % ATT_SETUP_ROWS == 0

    def seq_rows(width):
        return pl.BlockSpec((seq, width), lambda b: (b, 0))

    return pl.pallas_call(
        functools.partial(_attn_kernel, seq=seq, layer=l),
        grid=(batch,),
        in_specs=[seq_rows(ATT_WIDTH), seq_rows(KV_WIDTH), seq_rows(KV_WIDTH),
                  pl.BlockSpec(bias.shape, lambda b: (0, 0, 0), pipeline_mode=pl.Buffered(1)),
                  pl.BlockSpec(memory_space=pltpu.SMEM)],
        out_specs=seq_rows(ATT_WIDTH),
        out_shape=jax.ShapeDtypeStruct((batch * seq, ATT_WIDTH), BF16),
        scratch_shapes=[pltpu.VMEM((2 * N_KV_HEADS, seq + 2 * BLOCK, LANES), BF16)] * 2,
        compiler_params=pltpu.CompilerParams(dimension_semantics=("arbitrary",),
                                             vmem_limit_bytes=VMEM_LIMIT),
        name="windowed_gqa",
    )(q, k, v, bias, sink)


def _t5_bucket_index():
    t = jnp.arange(BLOCK)[:, None]
    j = jnp.arange(3 * BLOCK)[None, :]
    rel = j - BLOCK - t
    half = N_BUCKETS // 2
    max_exact = half // 2
    ret = (rel > 0).astype(jnp.int32) * half
    n = jnp.abs(rel)
    n_f = jnp.maximum(n, 1).astype(jnp.float32)
    large = max_exact + (jnp.log(n_f / max_exact) / math.log(MAX_DISTANCE / max_exact)
                         * (half - max_exact)).astype(jnp.int32)
    large = jnp.minimum(large, half - 1)
    return ret + jnp.where(n < max_exact, n, large)


def _gate_weights(w_a, b_a, w_x, b_x):
    depth = w_a.shape[0]
    n_slabs = LRU_WIDTH // LANES
    per_slab = LANES // LRU_BLOCK_W

    def slab_blockdiag(w):
        w = w.reshape(depth, n_slabs, per_slab, LRU_BLOCK_W, LRU_BLOCK_W)
        eye = jnp.eye(per_slab, dtype=w.dtype)
        full = jnp.einsum('lspcd,pq->lspcqd', w, eye)
        return full.reshape(depth, n_slabs, LANES, LANES)

    mats = [slab_blockdiag(w[:, d]) for d in range(2) for w in (w_a, w_x)]
    wg = (0.5 * jnp.concatenate(mats, axis=-1)).astype(BF16)
    biases = [b[:, d].reshape(depth, n_slabs, 1, LANES) for d in range(2) for b in (b_a, b_x)]
    bg = 0.5 * jnp.concatenate(biases, axis=-1)
    bg_hi = bg.astype(BF16)
    bg_lo = (bg - bg_hi.astype(F32)).astype(BF16)
    unused = jnp.zeros((depth, n_slabs, LANES - GATE_BIAS_ROWS, 4 * LANES), BF16)
    return jnp.concatenate([wg, bg_hi, bg_lo, unused], axis=2)


def kernel(x, ffn1_norm, ffn1_w_gate, ffn1_w_up, ffn1_w_down, mix_norm, w_in, conv_w, conv_b, lru_w_a, lru_b_a, lru_w_x, lru_b_x, lru_lambda, attn_sink, rel_bias, lru_out_norm, attn_out_norm, w_out, ffn2_norm, ffn2_w_gate, ffn2_w_up, ffn2_w_down, final_norm):
    batch, seq, d_model = x.shape
    depth = ffn1_norm.shape[0]
    tokens = batch * seq
    assert d_model == D_MODEL and seq % BLOCK == 0 and tokens % FFN1_TILE == 0 and tokens % FFN2_TILE == 0
    xt = x.reshape(tokens, D_MODEL)
    bias = _rel_bias_table(_t5_bucket_index(), rel_bias)
    gate_w = _gate_weights(lru_w_a, lru_b_a, lru_w_x, lru_b_x)
    cb = conv_b.reshape(depth, 1, -1)
    nf = final_norm.reshape(1, -1)
    mix_out_norm = jnp.concatenate([lru_out_norm, attn_out_norm], axis=-1)
    g_ffn1, g_mix, g_ffn2, g_out = jnp.stack([ffn1_norm, mix_norm, ffn2_norm, mix_out_norm])[..., None]
    half_step = jnp.full((depth, D_FF, 1), FFN_RES, F32)
    ffn1_stacks = (ffn1_w_gate, ffn1_w_up, ffn1_w_down, w_in)
    ffn1_gains = (g_ffn1, g_ffn1, half_step, g_mix)
    ffn2_stacks = (ffn2_w_gate, ffn2_w_up, ffn2_w_down, w_out)
    ffn2_gains = (g_ffn2, g_ffn2, half_step, g_out)
    ffn1_w = tuple((w[0] * g[0]).astype(BF16) for w, g in zip(ffn1_stacks, ffn1_gains))
    for l in range(depth):
        (xt, xr, gate, q, k, v), ffn2_w = _ffn1_inproj(l, xt, *ffn1_w, ffn2_stacks, ffn2_gains)
        last = l == depth - 1
        y_rec, ffn1_w = _rec(l, xr, gate, conv_w, cb, gate_w, lru_lambda, batch, seq,
                             () if last else ffn1_stacks, () if last else ffn1_gains, l + 1)
        y_att = _attn(l, q, k, v, bias, attn_sink, batch, seq)
        xt = _outproj_ffn2(xt, y_rec, y_att, ffn2_w[3], *ffn2_w[:3], nf, final_norm=last)
    return xt.reshape(batch, seq, D_MODEL)
```

```python
import functools
import math

import jax
import jax.numpy as jnp
from jax import lax
from jax.experimental import pallas as pl
from jax.experimental.pallas import tpu as pltpu

F32 = jnp.float32
BF16 = jnp.bfloat16

D_MODEL = 1024
LRU_WIDTH = 512
LRU_BLOCK_W = 64
LRU_C = 8.0
CONV_WIDTH = 4
N_HEADS = 8
N_KV_HEADS = 2
KV_GROUP = N_HEADS // N_KV_HEADS
HEAD_DIM = 64
ATT_WIDTH = N_HEADS * HEAD_DIM
KV_WIDTH = N_KV_HEADS * HEAD_DIM
WINDOW = 128
BLOCK = 128
N_BUCKETS = 32
MAX_DISTANCE = 128
D_FF = 2816
FFN_RES = 0.5
EPS = 1e-6
NEG_INF = -1e30
D_IN = 2 * LRU_WIDTH + ATT_WIDTH + 2 * KV_WIDTH
LOG2E = math.log2(math.e)

LANES = 128
SUBLANES = 8
FF_CHUNK = 256
N_FF_CHUNKS = D_FF // FF_CHUNK
FFN1_TILE = 512
FFN2_TILE = 1024
VMEM_LIMIT = 56 * 1024 * 1024

SCAN_SEGS = 32
SCAN_VREGS = SCAN_SEGS // SUBLANES
SEG_PITCH = 129
REC_CHUNK = 512
GATE_BIAS_ROWS = 2


def _rms(x, g):
    return x * lax.rsqrt(jnp.mean(x * x, axis=-1, keepdims=True) + EPS) * g


def _sigmoid(x):
    return 0.5 * jnp.tanh(0.5 * x) + 0.5


def _inv_rms(x):
    return lax.rsqrt(jnp.mean(x * x, axis=-1, keepdims=True) + EPS)


def _swiglu_into(x_ref, inv_rms, wg_ref, wu_ref, wd_ref, acc_ref, residual):
    scale = None if inv_rms is None else jnp.broadcast_to(inv_rms, (x_ref.shape[0], FF_CHUNK))
    for c in range(N_FF_CHUNKS):
        cols = slice(c * FF_CHUNK, (c + 1) * FF_CHUNK)
        g = jnp.dot(x_ref[...], wg_ref[:, cols], preferred_element_type=F32)
        u = jnp.dot(x_ref[...], wu_ref[:, cols], preferred_element_type=F32)
        if scale is not None:
            g, u = scale * g, scale * u
        act = (g * _sigmoid(g) * u).astype(BF16)
        part = jnp.dot(act, wd_ref[cols, :], preferred_element_type=F32)
        if c == 0:
            acc_ref[...] = residual + part
        else:
            acc_ref[...] += part


N_CAST = 4


def _cast_blocks(src_refs, dst_refs):
    n = len(dst_refs)
    for w, gain, dst in zip(src_refs[:n], src_refs[n:], dst_refs):
        dst[...] = (w[...] * gain[...]).astype(BF16)


def _ffn1_inproj_kernel(x_ref, wg_ref, wu_ref, wd_ref, win_ref, *rest):
    cast_src, rest = rest[:2 * N_CAST], rest[2 * N_CAST:]
    xo_ref, xr_ref, gate_ref, q_ref, k_ref, v_ref = rest[:6]
    cast_dst, (xb_ref, acc_ref) = rest[6:6 + N_CAST], rest[6 + N_CAST:]
    _cast_blocks(cast_src, cast_dst)
    x = x_ref[...]
    xb_ref[...] = x.astype(BF16)
    _swiglu_into(xb_ref, _inv_rms(x), wg_ref, wu_ref, wd_ref, acc_ref, x)
    x1 = acc_ref[...]
    xo_ref[...] = x1
    xb_ref[...] = x1.astype(BF16)
    inv1 = _inv_rms(x1)

    def proj(rows, lo, width):
        return inv1[rows] * jnp.dot(xb_ref[rows, :], win_ref[:, lo:lo + width], preferred_element_type=F32)

    every = slice(None)
    tile = 2 * LANES
    for lo in range(0, LRU_WIDTH, tile):
        gate_ref[:, lo:lo + tile] = jax.nn.gelu(proj(every, LRU_WIDTH + lo, tile))
        xr_ref[:, lo:lo + tile] = proj(every, lo, tile)
    o = 2 * LRU_WIDTH
    q_ref[...] = (proj(every, o, ATT_WIDTH) * (HEAD_DIM ** -0.5 * LOG2E)).astype(BF16)
    o += ATT_WIDTH
    half = x_ref.shape[0] // 2
    for rows in (slice(0, half), slice(half, 2 * half)):
        kv = proj(rows, o, 2 * KV_WIDTH)
        k_ref[rows, :] = kv[:, :KV_WIDTH].astype(BF16)
        v_ref[rows, :] = kv[:, KV_WIDTH:].astype(BF16)


def _outproj_ffn2_kernel(x_ref, yr_ref, ya_ref, wo_ref, wg_ref, wu_ref, wd_ref, nf_ref,
                         xo_ref, xb_ref, acc_ref, *, final_norm):
    yr = yr_ref[...]
    y = jnp.concatenate([(yr * _inv_rms(yr)).astype(BF16), ya_ref[...]], axis=1)
    x1 = x_ref[...] + jnp.dot(y, wo_ref[...], preferred_element_type=F32)
    xb_ref[...] = (x1 * _inv_rms(x1)).astype(BF16)
    _swiglu_into(xb_ref, None, wg_ref, wu_ref, wd_ref, acc_ref, x1)
    x2 = acc_ref[...]
    if final_norm:
        x2 = _rms(x2, nf_ref[...])
    xo_ref[...] = x2


def _whole(arr):
    assert arr.ndim == 2
    return pl.BlockSpec(arr.shape, lambda *_: (0, 0), pipeline_mode=pl.Buffered(1))


def _rows(width, tm):
    return pl.BlockSpec((tm, width), lambda i: (i, 0))


def _cast_plan(stacks, gains, l, n_steps, step_of):
    bf16_rows = 2 * SUBLANES
    w_specs, g_specs, out_specs, out_shapes = [], [], [], []
    for w, gain in zip(stacks, gains):
        _, r, c = w.shape
        assert gain.shape == (w.shape[0], r, 1)
        n_blocks = n_steps
        while r % n_blocks or (r // n_blocks) % bf16_rows:
            n_blocks //= 2
        assert n_blocks >= 1 and n_steps % n_blocks == 0
        rep, rows = n_steps // n_blocks, r // n_blocks
        w_specs.append(pl.BlockSpec((None, rows, c), lambda *g, rep=rep: (l, step_of(*g) // rep, 0)))
        g_specs.append(pl.BlockSpec((None, rows, 1), lambda *g, rep=rep: (l, step_of(*g) // rep, 0)))
        out_specs.append(pl.BlockSpec((rows, c), lambda *g, rep=rep: (step_of(*g) // rep, 0)))
        out_shapes.append(jax.ShapeDtypeStruct((r, c), BF16))
    return w_specs + g_specs, out_specs, out_shapes


def _ffn1_inproj(l, x, wg, wu, wd, win, cast_stacks, cast_gains):
    t = x.shape[0]
    tm = FFN1_TILE
    out_shape = [
        jax.ShapeDtypeStruct((t, D_MODEL), F32),
        jax.ShapeDtypeStruct((t, LRU_WIDTH), F32),
        jax.ShapeDtypeStruct((t, LRU_WIDTH), F32),
        jax.ShapeDtypeStruct((t, ATT_WIDTH), BF16),
        jax.ShapeDtypeStruct((t, KV_WIDTH), BF16),
        jax.ShapeDtypeStruct((t, KV_WIDTH), BF16),
    ]
    out_specs = [_rows(D_MODEL, tm), _rows(LRU_WIDTH, tm), _rows(LRU_WIDTH, tm), _rows(ATT_WIDTH, tm),
                 _rows(KV_WIDTH, tm), _rows(KV_WIDTH, tm)]
    cast_in, cast_out, cast_shapes = _cast_plan(cast_stacks, cast_gains, l, t // tm, lambda i: i)
    outs = pl.pallas_call(
        _ffn1_inproj_kernel,
        grid=(t // tm,),
        in_specs=[_rows(D_MODEL, tm), _whole(wg), _whole(wu), _whole(wd), _whole(win)] + cast_in,
        out_specs=out_specs + cast_out,
        out_shape=out_shape + cast_shapes,
        scratch_shapes=[pltpu.VMEM((tm, D_MODEL), BF16), pltpu.VMEM((tm, D_MODEL), F32)],
        compiler_params=pltpu.CompilerParams(dimension_semantics=("arbitrary",),
                                             vmem_limit_bytes=VMEM_LIMIT),
        name="ffn1_inproj",
    )(x, wg, wu, wd, win, *cast_stacks, *cast_gains)
    return outs[:6], outs[6:]


def _outproj_ffn2(x, yr, ya, wo, wg, wu, wd, nf, final_norm):
    t = x.shape[0]
    tm = FFN2_TILE
    return pl.pallas_call(
        functools.partial(_outproj_ffn2_kernel, final_norm=final_norm),
        grid=(t // tm,),
        in_specs=[_rows(D_MODEL, tm), _rows(LRU_WIDTH, tm), _rows(ATT_WIDTH, tm), _whole(wo),
                  _whole(wg), _whole(wu), _whole(wd), _whole(nf)],
        out_specs=_rows(D_MODEL, tm),
        out_shape=jax.ShapeDtypeStruct((t, D_MODEL), F32),
        scratch_shapes=[pltpu.VMEM((tm, D_MODEL), BF16), pltpu.VMEM((tm, D_MODEL), F32)],
        compiler_params=pltpu.CompilerParams(dimension_semantics=("arbitrary",),
                                             vmem_limit_bytes=VMEM_LIMIT),
        name="outproj_ffn2",
    )(x, yr, ya, wo, wg, wu, wd, nf)


def _rec_kernel(xr_ref, gate_ref, cw_ref, cb_ref, wg_ref, lam_ref, *rest, seq, n_cast):
    cast_src, o_ref = rest[:2 * n_cast], rest[2 * n_cast]
    cast_dst = rest[2 * n_cast + 1:3 * n_cast + 1]
    a_f, u_f, a_b, u_b, h_f, h_b = rest[3 * n_cast + 1:]
    _cast_blocks(cast_src, cast_dst)
    n_chunks = seq // REC_CHUNK
    n_steps = SEG_PITCH
    pad_rows = SCAN_SEGS * SEG_PITCH - seq
    chunk_row = lax.broadcasted_iota(jnp.int32, (REC_CHUNK, LANES), 0)

    def shifted_rows(c, offset):
        lo = c * REC_CHUNK + offset
        if lo >= 0 and lo + REC_CHUNK <= seq:
            return xr_ref[lo:lo + REC_CHUNK, :]
        rolled = pltpu.roll(xr_ref[c * REC_CHUNK:(c + 1) * REC_CHUNK, :], (-offset) % REC_CHUNK, 0)
        outside = (chunk_row < -offset) if offset < 0 else (chunk_row >= REC_CHUNK - offset)
        return jnp.where(outside, 0.0, rolled)

    for a_ref, u_ref in ((a_f, u_f), (a_b, u_b)):
        a_ref[seq:seq + pad_rows, :] = jnp.ones((pad_rows, LANES), F32)
        u_ref[seq:seq + pad_rows, :] = jnp.zeros((pad_rows, LANES), F32)

    lam = lam_ref[...]
    neg_softplus = -(jnp.maximum(-lam, 0.0) + jnp.log1p(jnp.exp(-jnp.abs(lam))))
    half_log2_a = (0.5 * LRU_C * LOG2E) * neg_softplus

    cw = cw_ref[...]
    cb = cb_ref[...]
    bias_cols = (lax.broadcasted_iota(jnp.int32, (REC_CHUNK, LANES), 1) < GATE_BIAS_ROWS).astype(BF16)
    for c in range(n_chunks):
        xc = cb
        for tap in range(CONV_WIDTH):
            xc = xc + cw[tap:tap + 1, :] * shifted_rows(c, tap - CONV_WIDTH // 2)
        lhs = jnp.concatenate([xc.astype(BF16), bias_cols], axis=1)
        th = jnp.tanh(jnp.dot(lhs, wg_ref[...], preferred_element_type=F32))
        half_xc = 0.5 * xc
        rows = slice(c * REC_CHUNK, (c + 1) * REC_CHUNK)
        for d, (a_ref, u_ref) in enumerate(((a_f, u_f), (a_b, u_b))):
            th_r = th[:, (2 * d) * LANES:(2 * d + 1) * LANES]
            th_i = th[:, (2 * d + 1) * LANES:(2 * d + 2) * LANES]
            scale = half_log2_a[d:d + 1, :]
            a = jnp.exp2(scale * th_r + scale)
            v = 1.0 - a * a
            root = jnp.where(v > 0.0, v * lax.rsqrt(v), 0.0)
            a_ref[rows, :] = a
            u_ref[rows, :] = root * (th_i * half_xc + half_xc)

    def seg_rows(j):
        return pl.ds(j, SCAN_SEGS, stride=SEG_PITCH)

    zeros = jnp.zeros((SCAN_SEGS, LANES), F32)
    ones = jnp.ones((SCAN_SEGS, LANES), F32)

    def segment_maps(j, carry):
        hf, pf, hb, pb = carry
        jb = n_steps - 1 - j
        af = a_f[seg_rows(j), :]
        ab = a_b[seg_rows(jb), :]
        hf = af * hf + u_f[seg_rows(j), :]
        hb = ab * hb + u_b[seg_rows(jb), :]
        return hf, pf * af, hb, pb * ab

    hf, pf, hb, pb = lax.fori_loop(0, n_steps, segment_maps, (zeros, ones, zeros, ones), unroll=4)

    seg = lax.broadcasted_iota(jnp.int32, (SCAN_SEGS, LANES), 0)

    def entering(h, p, toward_higher):
        d = 1
        while d < SCAN_SEGS:
            shift = d if toward_higher else SCAN_SEGS - d
            have = (seg >= d) if toward_higher else (seg < SCAN_SEGS - d)
            h = jnp.where(have, h + p * pltpu.roll(h, shift, 0), h)
            p = jnp.where(have, p * pltpu.roll(p, shift, 0), p)
            d *= 2
        one_step = 1 if toward_higher else SCAN_SEGS - 1
        edge = (seg == 0) if toward_higher else (seg == SCAN_SEGS - 1)
        return jnp.where(edge, 0.0, pltpu.roll(h, one_step, 0))

    hf0 = entering(hf, pf, True)
    hb0 = entering(hb, pb, False)

    def true_states(j, carry):
        hf, hb = carry
        jb = n_steps - 1 - j
        hf = a_f[seg_rows(j), :] * hf + u_f[seg_rows(j), :]
        hb = a_b[seg_rows(jb), :] * hb + u_b[seg_rows(jb), :]
        h_f[seg_rows(j), :] = hf
        h_b[seg_rows(jb), :] = hb
        return hf, hb

    lax.fori_loop(0, n_steps, true_states, (hf0, hb0), unroll=4)

    for c in range(n_chunks):
        rows = slice(c * REC_CHUNK, (c + 1) * REC_CHUNK)
        o_ref[rows, :] = gate_ref[rows, :] * (h_f[rows, :] + h_b[rows, :])


def _rec(l, xr, gate, cw, cb, wg, lam, batch, seq, cast_stacks, cast_gains, cast_layer):
    n_slabs = LRU_WIDTH // LANES
    assert SCAN_SEGS * SEG_PITCH >= seq > (SCAN_SEGS - 1) * SEG_PITCH and seq % REC_CHUNK == 0
    slab = pl.BlockSpec((seq, LANES), lambda b, s: (b, s))
    scan_rows = SCAN_SEGS * SEG_PITCH
    cast_in, cast_out, cast_shapes = _cast_plan(cast_stacks, cast_gains, cast_layer, batch * n_slabs,
                                                lambda b, s: b * n_slabs + s)
    outs = pl.pallas_call(
        functools.partial(_rec_kernel, seq=seq, n_cast=len(cast_stacks)),
        grid=(batch, n_slabs),
        in_specs=[slab, slab,
                  pl.BlockSpec((None, CONV_WIDTH, LANES), lambda b, s: (l, 0, s)),
                  pl.BlockSpec((None, 1, LANES), lambda b, s: (l, 0, s)),
                  pl.BlockSpec((None, None, 2 * LANES, 4 * LANES), lambda b, s: (l, s, 0, 0)),
                  pl.BlockSpec((None, 2, LANES), lambda b, s: (l, 0, s))] + cast_in,
        out_specs=[slab] + cast_out,
        out_shape=[jax.ShapeDtypeStruct((batch * seq, LRU_WIDTH), F32)] + cast_shapes,
        scratch_shapes=[pltpu.VMEM((scan_rows, LANES), F32)] * 6,
        compiler_params=pltpu.CompilerParams(dimension_semantics=("arbitrary", "arbitrary"),
                                             vmem_limit_bytes=VMEM_LIMIT),
        name="rg_lru",
    )(xr, gate, cw, cb, wg, lam, *cast_stacks, *cast_gains)
    return outs[0], outs[1:]


def _bias_kernel(buckets_ref, table_ref, o_ref):
    buckets = buckets_ref[...]
    for h in range(N_HEADS):
        acc = jnp.zeros(buckets.shape, F32)
        for b in range(N_BUCKETS):
            acc = jnp.where(buckets == b, table_ref[b, h] * LOG2E, acc)
        g, p, e = h // KV_GROUP, (h % KV_GROUP) // 2, h % 2
        o_ref[2 * g + e, p * BLOCK:(p + 1) * BLOCK, :] = acc


def _rel_bias_table(buckets, rel_bias):
    return pl.pallas_call(
        _bias_kernel,
        in_specs=[pl.BlockSpec(memory_space=pltpu.VMEM), pl.BlockSpec(memory_space=pltpu.SMEM)],
        out_specs=pl.BlockSpec(memory_space=pltpu.VMEM),
        out_shape=jax.ShapeDtypeStruct((N_HEADS // 2, 2 * BLOCK, 3 * BLOCK), F32),
        name="rel_bias",
    )(buckets, rel_bias)


ATT_SETUP_ROWS = 512
ATT_BLOCKS = 1
ATT_UNROLL = 4
HALF = LANES // 2


def _attn_kernel(q_ref, k_ref, v_ref, bias_ref, sink_ref, o_ref, kvar, vvar, *, seq, layer):
    nb = seq // BLOCK
    lo_half = lax.broadcasted_iota(jnp.int32, (ATT_SETUP_ROWS, LANES), 1) < HALF
    zeros_blk = jnp.zeros((BLOCK, LANES), BF16)
    for u in range(2 * N_KV_HEADS):
        for ref in (kvar, vvar):
            ref[u, 0:BLOCK, :] = zeros_blk
            ref[u, BLOCK + seq:2 * BLOCK + seq, :] = zeros_blk
    for r in range(seq // ATT_SETUP_ROWS):
        src = slice(r * ATT_SETUP_ROWS, (r + 1) * ATT_SETUP_ROWS)
        dst = slice(BLOCK + r * ATT_SETUP_ROWS, BLOCK + (r + 1) * ATT_SETUP_ROWS)
        kc = k_ref[src, :]
        vc = v_ref[src, :]
        kr = pltpu.roll(kc, HALF, 1)
        vr = pltpu.roll(vc, HALF, 1)
        zero = jnp.zeros_like(kc)
        one = jnp.ones_like(vc)
        kvar[0, dst, :] = jnp.where(lo_half, kc, zero)
        kvar[1, dst, :] = jnp.where(lo_half, zero, kr)
        kvar[2, dst, :] = jnp.where(lo_half, kr, zero)
        kvar[3, dst, :] = jnp.where(lo_half, zero, kc)
        vvar[0, dst, :] = jnp.where(lo_half, vc, one)
        vvar[1, dst, :] = jnp.where(lo_half, one, vr)
        vvar[2, dst, :] = jnp.where(lo_half, vr, one)
        vvar[3, dst, :] = jnp.where(lo_half, one, vc)

    assert BLOCK == WINDOW == LANES
    t_idx = lax.broadcasted_iota(jnp.int32, (2 * BLOCK, LANES), 0) % BLOCK
    lane_minus_t = lax.broadcasted_iota(jnp.int32, (2 * BLOCK, LANES), 1) - t_idx
    first_head = lax.broadcasted_iota(jnp.int32, (2 * BLOCK, 1), 0) < BLOCK
    lo_out = lax.broadcasted_iota(jnp.int32, (BLOCK, LANES), 1) < HALF
    units = [(g, e) for g in range(N_KV_HEADS) for e in range(2)]
    never = 2 * LANES

    def blocks(i, carry):
        ns = [i * ATT_BLOCKS + j for j in range(ATT_BLOCKS)]
        starts = [pl.multiple_of(n * BLOCK, BLOCK) for n in ns]
        work = [(j, g, e) for j in range(ATT_BLOCKS) for g, e in units]
        scores = []
        for j, g, e in work:
            qblk = q_ref[pl.ds(starts[j], BLOCK), (2 * g) * LANES:(2 * g + 2) * LANES]
            lhs = jnp.concatenate([qblk[:, :LANES], qblk[:, LANES:]], axis=0)
            kwin = kvar[2 * g + e, pl.ds(starts[j], 3 * BLOCK), :]
            scores.append(lax.dot_general(lhs, kwin, (((1,), (1,)), ((), ())),
                                          preferred_element_type=F32))
        probs, sink_mass = [], []
        for (j, g, e), s in zip(work, scores):
            u = 2 * g + e
            prev_min = jnp.where(ns[j] == 0, never, 0)
            next_max = jnp.where(ns[j] == nb - 1, -never, 0)
            s_prev = jnp.where(lane_minus_t >= prev_min, s[:, :BLOCK] + bias_ref[u, :, :BLOCK], NEG_INF)
            s_own = s[:, BLOCK:2 * BLOCK] + bias_ref[u, :, BLOCK:2 * BLOCK]
            s_next = jnp.where(lane_minus_t <= next_max, s[:, 2 * BLOCK:] + bias_ref[u, :, 2 * BLOCK:], NEG_INF)
            sink = jnp.where(first_head, sink_ref[layer, 4 * g + e], sink_ref[layer, 4 * g + 2 + e]) * LOG2E
            row_max = jnp.max(jnp.maximum(jnp.maximum(s_prev, s_own), s_next), axis=-1, keepdims=True)
            m = jnp.maximum(row_max, sink)
            p = jnp.concatenate([jnp.exp2(s_prev - m), jnp.exp2(s_own - m), jnp.exp2(s_next - m)], axis=-1)
            probs.append(p.astype(BF16))
            sink_mass.append(jnp.exp2(sink - m))
        outs = []
        for (j, g, e), p in zip(work, probs):
            vwin = vvar[2 * g + e, pl.ds(starts[j], 3 * BLOCK), :]
            outs.append(jnp.dot(p, vwin, preferred_element_type=F32))
        for j in range(ATT_BLOCKS):
            chunks = []
            for g in range(N_KV_HEADS):
                for p in range(2):
                    rows = slice(p * BLOCK, (p + 1) * BLOCK)
                    ue, uo = j * len(units) + 2 * g, j * len(units) + 2 * g + 1
                    even, odd = outs[ue][rows], outs[uo][rows]
                    num = jnp.where(lo_out, even, odd)
                    den = (pltpu.roll(jnp.where(lo_out, odd, even), HALF, 1)
                           + jnp.where(lo_out, sink_mass[ue][rows], sink_mass[uo][rows]))
                    chunks.append(num / den)
            y = jnp.concatenate(chunks, axis=-1)
            o_ref[pl.ds(starts[j], BLOCK), :] = (y * _inv_rms(y)).astype(BF16)
        return carry

    lax.fori_loop(0, nb // ATT_BLOCKS, blocks, 0, unroll=ATT_UNROLL)


def _attn(l, q, k, v, bias, sink, batch, seq):
    assert seq % ATT_SETUP_ROWS == 0

    def seq_rows(width):
        return pl.BlockSpec((seq, width), lambda b: (b, 0))

    return pl.pallas_call(
        functools.partial(_attn_kernel, seq=seq, layer=l),
        grid=(batch,),
        in_specs=[seq_rows(ATT_WIDTH), seq_rows(KV_WIDTH), seq_rows(KV_WIDTH),
                  pl.BlockSpec(bias.shape, lambda b: (0, 0, 0), pipeline_mode=pl.Buffered(1)),
                  pl.BlockSpec(memory_space=pltpu.SMEM)],
        out_specs=seq_rows(ATT_WIDTH),
        out_shape=jax.ShapeDtypeStruct((batch * seq, ATT_WIDTH), BF16),
        scratch_shapes=[pltpu.VMEM((2 * N_KV_HEADS, seq + 2 * BLOCK, LANES), BF16)] * 2,
        compiler_params=pltpu.CompilerParams(dimension_semantics=("arbitrary",),
                                             vmem_limit_bytes=VMEM_LIMIT),
        name="windowed_gqa",
    )(q, k, v, bias, sink)


def _t5_bucket_index():
    t = jnp.arange(BLOCK)[:, None]
    j = jnp.arange(3 * BLOCK)[None, :]
    rel = j - BLOCK - t
    half = N_BUCKETS // 2
    max_exact = half // 2
    ret = (rel > 0).astype(jnp.int32) * half
    n = jnp.abs(rel)
    n_f = jnp.maximum(n, 1).astype(jnp.float32)
    large = max_exact + (jnp.log(n_f / max_exact) / math.log(MAX_DISTANCE / max_exact)
                         * (half - max_exact)).astype(jnp.int32)
    large = jnp.minimum(large, half - 1)
    return ret + jnp.where(n < max_exact, n, large)


def _gate_weights(w_a, b_a, w_x, b_x):
    depth = w_a.shape[0]
    n_slabs = LRU_WIDTH // LANES
    per_slab = LANES // LRU_BLOCK_W

    def slab_blockdiag(w):
        w = w.reshape(depth, n_slabs, per_slab, LRU_BLOCK_W, LRU_BLOCK_W)
        eye = jnp.eye(per_slab, dtype=w.dtype)
        full = jnp.einsum('lspcd,pq->lspcqd', w, eye)
        return full.reshape(depth, n_slabs, LANES, LANES)

    mats = [slab_blockdiag(w[:, d]) for d in range(2) for w in (w_a, w_x)]
    wg = (0.5 * jnp.concatenate(mats, axis=-1)).astype(BF16)
    biases = [b[:, d].reshape(depth, n_slabs, 1, LANES) for d in range(2) for b in (b_a, b_x)]
    bg = 0.5 * jnp.concatenate(biases, axis=-1)
    bg_hi = bg.astype(BF16)
    bg_lo = (bg - bg_hi.astype(F32)).astype(BF16)
    unused = jnp.zeros((depth, n_slabs, LANES - GATE_BIAS_ROWS, 4 * LANES), BF16)
    return jnp.concatenate([wg, bg_hi, bg_lo, unused], axis=2)


def kernel(x, ffn1_norm, ffn1_w_gate, ffn1_w_up, ffn1_w_down, mix_norm, w_in, conv_w, conv_b, lru_w_a, lru_b_a, lru_w_x, lru_b_x, lru_lambda, attn_sink, rel_bias, lru_out_norm, attn_out_norm, w_out, ffn2_norm, ffn2_w_gate, ffn2_w_up, ffn2_w_down, final_norm):
    batch, seq, d_model = x.shape
    depth = ffn1_norm.shape[0]
    tokens = batch * seq
    assert d_model == D_MODEL and seq % BLOCK == 0 and tokens % FFN1_TILE == 0 and tokens % FFN2_TILE == 0
    xt = x.reshape(tokens, D_MODEL)
    bias = _rel_bias_table(_t5_bucket_index(), rel_bias)
    gate_w = _gate_weights(lru_w_a, lru_b_a, lru_w_x, lru_b_x)
    cb = conv_b.reshape(depth, 1, -1)
    nf = final_norm.reshape(1, -1)
    mix_out_norm = jnp.concatenate([lru_out_norm, attn_out_norm], axis=-1)
    g_ffn1, g_mix, g_ffn2, g_out = jnp.stack([ffn1_norm, mix_norm, ffn2_norm, mix_out_norm])[..., None]
    half_step = jnp.full((depth, D_FF, 1), FFN_RES, F32)
    ffn1_stacks = (ffn1_w_gate, ffn1_w_up, ffn1_w_down, w_in)
    ffn1_gains = (g_ffn1, g_ffn1, half_step, g_mix)
    ffn2_stacks = (ffn2_w_gate, ffn2_w_up, ffn2_w_down, w_out)
    ffn2_gains = (g_ffn2, g_ffn2, half_step, g_out)
    ffn1_w = tuple((w[0] * g[0]).astype(BF16) for w, g in zip(ffn1_stacks, ffn1_gains))
    for l in range(depth):
        (xt, xr, gate, q, k, v), ffn2_w = _ffn1_inproj(l, xt, *ffn1_w, ffn2_stacks, ffn2_gains)
        last = l == depth - 1
        y_rec, ffn1_w = _rec(l, xr, gate, conv_w, cb, gate_w, lru_lambda, batch, seq,
                             () if last else ffn1_stacks, () if last else ffn1_gains, l + 1)
        y_att = _attn(l, q, k, v, bias, attn_sink, batch, seq)
        xt = _outproj_ffn2(xt, y_rec, y_att, ffn2_w[3], *ffn2_w[:3], nf, final_norm=last)
    return xt.reshape(batch, seq, D_MODEL)
```

```python
import functools
import math

import jax
import jax.numpy as jnp
from jax import lax
from jax.experimental import pallas as pl
from jax.experimental.pallas import tpu as pltpu

F32 = jnp.float32
BF16 = jnp.bfloat16

D_MODEL = 1024
LRU_WIDTH = 512
LRU_BLOCK_W = 64
LRU_C = 8.0
CONV_WIDTH = 4
N_HEADS = 8
N_KV_HEADS = 2
KV_GROUP = N_HEADS // N_KV_HEADS
HEAD_DIM = 64
ATT_WIDTH = N_HEADS * HEAD_DIM
KV_WIDTH = N_KV_HEADS * HEAD_DIM
WINDOW = 128
BLOCK = 128
N_BUCKETS = 32
MAX_DISTANCE = 128
D_FF = 2816
FFN_RES = 0.5
EPS = 1e-6
NEG_INF = -1e30
D_IN = 2 * LRU_WIDTH + ATT_WIDTH + 2 * KV_WIDTH
LOG2E = math.log2(math.e)

LANES = 128
SUBLANES = 8
FF_CHUNK = 256
N_FF_CHUNKS = D_FF // FF_CHUNK
FFN1_TILE = 512
FFN2_TILE = 1024
VMEM_LIMIT = 56 * 1024 * 1024

SCAN_SEGS = 32
SCAN_VREGS = SCAN_SEGS // SUBLANES
SEG_PITCH = 129
REC_CHUNK = 512
GATE_BIAS_ROWS = 2


def _rms(x, g):
    return x * lax.rsqrt(jnp.mean(x * x, axis=-1, keepdims=True) + EPS) * g


def _silu_of_twice(half):
    return half + half * jnp.tanh(half)


def _inv_rms(x):
    return lax.rsqrt(jnp.mean(x * x, axis=-1, keepdims=True) + EPS)


def _swiglu_into(xn_ref, wg_ref, wu_ref, wd_ref, acc_ref, residual):
    for c in range(N_FF_CHUNKS):
        cols = slice(c * FF_CHUNK, (c + 1) * FF_CHUNK)
        half_g = jnp.dot(xn_ref[...], wg_ref[:, cols], preferred_element_type=F32)
        u = jnp.dot(xn_ref[...], wu_ref[:, cols], preferred_element_type=F32)
        act = (_silu_of_twice(half_g) * u).astype(BF16)
        part = jnp.dot(act, wd_ref[cols, :], preferred_element_type=F32)
        if c == 0:
            acc_ref[...] = residual + part
        else:
            acc_ref[...] += part


N_CAST = 4


def _cast_blocks(src_refs, dst_refs):
    n = len(dst_refs)
    for w, gain, dst in zip(src_refs[:n], src_refs[n:], dst_refs):
        dst[...] = (w[...] * gain[...]).astype(BF16)


def _ffn1_inproj_kernel(x_ref, wg_ref, wu_ref, wd_ref, win_ref, *rest):
    cast_src, rest = rest[:2 * N_CAST], rest[2 * N_CAST:]
    xo_ref, xr_ref, gate_ref, q_ref, k_ref, v_ref = rest[:6]
    cast_dst, (xb_ref, acc_ref) = rest[6:6 + N_CAST], rest[6 + N_CAST:]
    _cast_blocks(cast_src, cast_dst)
    x = x_ref[...]
    xb_ref[...] = (x * _inv_rms(x)).astype(BF16)
    _swiglu_into(xb_ref, wg_ref, wu_ref, wd_ref, acc_ref, x)
    x1 = acc_ref[...]
    xo_ref[...] = x1
    xb_ref[...] = x1.astype(BF16)
    inv1 = _inv_rms(x1)

    def proj(rows, lo, width):
        return inv1[rows] * jnp.dot(xb_ref[rows, :], win_ref[:, lo:lo + width], preferred_element_type=F32)

    every = slice(None)
    tile = 2 * LANES
    for lo in range(0, LRU_WIDTH, tile):
        gate_ref[:, lo:lo + tile] = jax.nn.gelu(proj(every, LRU_WIDTH + lo, tile))
        xr_ref[:, lo:lo + tile] = proj(every, lo, tile)
    o = 2 * LRU_WIDTH
    q_ref[...] = (proj(every, o, ATT_WIDTH) * (HEAD_DIM ** -0.5 * LOG2E)).astype(BF16)
    o += ATT_WIDTH
    half = x_ref.shape[0] // 2
    for rows in (slice(0, half), slice(half, 2 * half)):
        kv = proj(rows, o, 2 * KV_WIDTH)
        k_ref[rows, :] = kv[:, :KV_WIDTH].astype(BF16)
        v_ref[rows, :] = kv[:, KV_WIDTH:].astype(BF16)


def _outproj_ffn2_kernel(x_ref, yr_ref, ya_ref, wo_ref, wg_ref, wu_ref, wd_ref, nf_ref,
                         xo_ref, xb_ref, acc_ref, *, final_norm):
    yr = yr_ref[...]
    y = jnp.concatenate([(yr * _inv_rms(yr)).astype(BF16), ya_ref[...]], axis=1)
    x1 = x_ref[...] + jnp.dot(y, wo_ref[...], preferred_element_type=F32)
    xb_ref[...] = (x1 * _inv_rms(x1)).astype(BF16)
    _swiglu_into(xb_ref, wg_ref, wu_ref, wd_ref, acc_ref, x1)
    x2 = acc_ref[...]
    if final_norm:
        x2 = _rms(x2, nf_ref[...])
    xo_ref[...] = x2


def _whole(arr):
    assert arr.ndim == 2
    return pl.BlockSpec(arr.shape, lambda *_: (0, 0), pipeline_mode=pl.Buffered(1))


def _rows(width, tm):
    return pl.BlockSpec((tm, width), lambda i: (i, 0))


def _cast_plan(stacks, gains, l, n_steps, step_of):
    bf16_rows = 2 * SUBLANES
    w_specs, g_specs, out_specs, out_shapes = [], [], [], []
    for w, gain in zip(stacks, gains):
        _, r, c = w.shape
        assert gain.shape == (w.shape[0], r, 1)
        n_blocks = n_steps
        while r % n_blocks or (r // n_blocks) % bf16_rows:
            n_blocks //= 2
        assert n_blocks >= 1 and n_steps % n_blocks == 0
        rep, rows = n_steps // n_blocks, r // n_blocks
        w_specs.append(pl.BlockSpec((None, rows, c), lambda *g, rep=rep: (l, step_of(*g) // rep, 0)))
        g_specs.append(pl.BlockSpec((None, rows, 1), lambda *g, rep=rep: (l, step_of(*g) // rep, 0)))
        out_specs.append(pl.BlockSpec((rows, c), lambda *g, rep=rep: (step_of(*g) // rep, 0)))
        out_shapes.append(jax.ShapeDtypeStruct((r, c), BF16))
    return w_specs + g_specs, out_specs, out_shapes


def _ffn1_inproj(l, x, wg, wu, wd, win, cast_stacks, cast_gains):
    t = x.shape[0]
    tm = FFN1_TILE
    out_shape = [
        jax.ShapeDtypeStruct((t, D_MODEL), F32),
        jax.ShapeDtypeStruct((t, LRU_WIDTH), F32),
        jax.ShapeDtypeStruct((t, LRU_WIDTH), F32),
        jax.ShapeDtypeStruct((t, ATT_WIDTH), BF16),
        jax.ShapeDtypeStruct((t, KV_WIDTH), BF16),
        jax.ShapeDtypeStruct((t, KV_WIDTH), BF16),
    ]
    out_specs = [_rows(D_MODEL, tm), _rows(LRU_WIDTH, tm), _rows(LRU_WIDTH, tm), _rows(ATT_WIDTH, tm),
                 _rows(KV_WIDTH, tm), _rows(KV_WIDTH, tm)]
    cast_in, cast_out, cast_shapes = _cast_plan(cast_stacks, cast_gains, l, t // tm, lambda i: i)
    outs = pl.pallas_call(
        _ffn1_inproj_kernel,
        grid=(t // tm,),
        in_specs=[_rows(D_MODEL, tm), _whole(wg), _whole(wu), _whole(wd), _whole(win)] + cast_in,
        out_specs=out_specs + cast_out,
        out_shape=out_shape + cast_shapes,
        scratch_shapes=[pltpu.VMEM((tm, D_MODEL), BF16), pltpu.VMEM((tm, D_MODEL), F32)],
        compiler_params=pltpu.CompilerParams(dimension_semantics=("arbitrary",),
                                             vmem_limit_bytes=VMEM_LIMIT),
        name="ffn1_inproj",
    )(x, wg, wu, wd, win, *cast_stacks, *cast_gains)
    return outs[:6], outs[6:]


def _outproj_ffn2(x, yr, ya, wo, wg, wu, wd, nf, final_norm):
    t = x.shape[0]
    tm = FFN2_TILE
    return pl.pallas_call(
        functools.partial(_outproj_ffn2_kernel, final_norm=final_norm),
        grid=(t // tm,),
        in_specs=[_rows(D_MODEL, tm), _rows(LRU_WIDTH, tm), _rows(ATT_WIDTH, tm), _whole(wo),
                  _whole(wg), _whole(wu), _whole(wd), _whole(nf)],
        out_specs=_rows(D_MODEL, tm),
        out_shape=jax.ShapeDtypeStruct((t, D_MODEL), F32),
        scratch_shapes=[pltpu.VMEM((tm, D_MODEL), BF16), pltpu.VMEM((tm, D_MODEL), F32)],
        compiler_params=pltpu.CompilerParams(dimension_semantics=("arbitrary",),
                                             vmem_limit_bytes=VMEM_LIMIT),
        name="outproj_ffn2",
    )(x, yr, ya, wo, wg, wu, wd, nf)


def _rec_kernel(xr_ref, gate_ref, cw_ref, cb_ref, wg_ref, lam_ref, *rest, seq, n_cast):
    cast_src, o_ref = rest[:2 * n_cast], rest[2 * n_cast]
    cast_dst = rest[2 * n_cast + 1:3 * n_cast + 1]
    a_f, u_f, a_b, u_b, h_f, h_b = rest[3 * n_cast + 1:]
    _cast_blocks(cast_src, cast_dst)
    n_chunks = seq // REC_CHUNK
    n_steps = SEG_PITCH
    pad_rows = SCAN_SEGS * SEG_PITCH - seq
    chunk_row = lax.broadcasted_iota(jnp.int32, (REC_CHUNK, LANES), 0)

    def shifted_rows(c, offset):
        lo = c * REC_CHUNK + offset
        if lo >= 0 and lo + REC_CHUNK <= seq:
            return xr_ref[lo:lo + REC_CHUNK, :]
        rolled = pltpu.roll(xr_ref[c * REC_CHUNK:(c + 1) * REC_CHUNK, :], (-offset) % REC_CHUNK, 0)
        outside = (chunk_row < -offset) if offset < 0 else (chunk_row >= REC_CHUNK - offset)
        return jnp.where(outside, 0.0, rolled)

    for a_ref, u_ref in ((a_f, u_f), (a_b, u_b)):
        a_ref[seq:seq + pad_rows, :] = jnp.ones((pad_rows, LANES), F32)
        u_ref[seq:seq + pad_rows, :] = jnp.zeros((pad_rows, LANES), F32)

    lam = lam_ref[...]
    neg_softplus = -(jnp.maximum(-lam, 0.0) + jnp.log1p(jnp.exp(-jnp.abs(lam))))
    half_log2_a = (0.5 * LRU_C * LOG2E) * neg_softplus

    cw = cw_ref[...]
    cb = cb_ref[...]
    bias_cols = (lax.broadcasted_iota(jnp.int32, (REC_CHUNK, LANES), 1) < GATE_BIAS_ROWS).astype(BF16)
    for c in range(n_chunks):
        xc = cb
        for tap in range(CONV_WIDTH):
            xc = xc + cw[tap:tap + 1, :] * shifted_rows(c, tap - CONV_WIDTH // 2)
        lhs = jnp.concatenate([xc.astype(BF16), bias_cols], axis=1)
        th = jnp.tanh(jnp.dot(lhs, wg_ref[...], preferred_element_type=F32))
        half_xc = 0.5 * xc
        rows = slice(c * REC_CHUNK, (c + 1) * REC_CHUNK)
        for d, (a_ref, u_ref) in enumerate(((a_f, u_f), (a_b, u_b))):
            th_r = th[:, (2 * d) * LANES:(2 * d + 1) * LANES]
            th_i = th[:, (2 * d + 1) * LANES:(2 * d + 2) * LANES]
            scale = half_log2_a[d:d + 1, :]
            a = jnp.exp2(scale * th_r + scale)
            v = 1.0 - a * a
            root = jnp.where(v > 0.0, v * lax.rsqrt(v), 0.0)
            a_ref[rows, :] = a
            u_ref[rows, :] = root * (th_i * half_xc + half_xc)

    def seg_rows(j):
        return pl.ds(j, SCAN_SEGS, stride=SEG_PITCH)

    zeros = jnp.zeros((SCAN_SEGS, LANES), F32)
    ones = jnp.ones((SCAN_SEGS, LANES), F32)

    def segment_maps(j, carry):
        hf, pf, hb, pb = carry
        jb = n_steps - 1 - j
        af = a_f[seg_rows(j), :]
        ab = a_b[seg_rows(jb), :]
        hf = af * hf + u_f[seg_rows(j), :]
        hb = ab * hb + u_b[seg_rows(jb), :]
        return hf, pf * af, hb, pb * ab

    hf, pf, hb, pb = lax.fori_loop(0, n_steps, segment_maps, (zeros, ones, zeros, ones), unroll=4)

    seg = lax.broadcasted_iota(jnp.int32, (SCAN_SEGS, LANES), 0)

    def entering(h, p, toward_higher):
        d = 1
        while d < SCAN_SEGS:
            shift = d if toward_higher else SCAN_SEGS - d
            have = (seg >= d) if toward_higher else (seg < SCAN_SEGS - d)
            h = jnp.where(have, h + p * pltpu.roll(h, shift, 0), h)
            p = jnp.where(have, p * pltpu.roll(p, shift, 0), p)
            d *= 2
        one_step = 1 if toward_higher else SCAN_SEGS - 1
        edge = (seg == 0) if toward_higher else (seg == SCAN_SEGS - 1)
        return jnp.where(edge, 0.0, pltpu.roll(h, one_step, 0))

    hf0 = entering(hf, pf, True)
    hb0 = entering(hb, pb, False)

    def true_states(j, carry):
        hf, hb = carry
        jb = n_steps - 1 - j
        hf = a_f[seg_rows(j), :] * hf + u_f[seg_rows(j), :]
        hb = a_b[seg_rows(jb), :] * hb + u_b[seg_rows(jb), :]
        h_f[seg_rows(j), :] = hf
        h_b[seg_rows(jb), :] = hb
        return hf, hb

    lax.fori_loop(0, n_steps, true_states, (hf0, hb0), unroll=4)

    for c in range(n_chunks):
        rows = slice(c * REC_CHUNK, (c + 1) * REC_CHUNK)
        o_ref[rows, :] = gate_ref[rows, :] * (h_f[rows, :] + h_b[rows, :])


def _rec(l, xr, gate, cw, cb, wg, lam, batch, seq, cast_stacks, cast_gains, cast_layer):
    n_slabs = LRU_WIDTH // LANES
    assert SCAN_SEGS * SEG_PITCH >= seq > (SCAN_SEGS - 1) * SEG_PITCH and seq % REC_CHUNK == 0
    slab = pl.BlockSpec((seq, LANES), lambda b, s: (b, s))
    scan_rows = SCAN_SEGS * SEG_PITCH
    cast_in, cast_out, cast_shapes = _cast_plan(cast_stacks, cast_gains, cast_layer, batch * n_slabs,
                                                lambda b, s: b * n_slabs + s)
    outs = pl.pallas_call(
        functools.partial(_rec_kernel, seq=seq, n_cast=len(cast_stacks)),
        grid=(batch, n_slabs),
        in_specs=[slab, slab,
                  pl.BlockSpec((None, CONV_WIDTH, LANES), lambda b, s: (l, 0, s)),
                  pl.BlockSpec((None, 1, LANES), lambda b, s: (l, 0, s)),
                  pl.BlockSpec((None, None, 2 * LANES, 4 * LANES), lambda b, s: (l, s, 0, 0)),
                  pl.BlockSpec((None, 2, LANES), lambda b, s: (l, 0, s))] + cast_in,
        out_specs=[slab] + cast_out,
        out_shape=[jax.ShapeDtypeStruct((batch * seq, LRU_WIDTH), F32)] + cast_shapes,
        scratch_shapes=[pltpu.VMEM((scan_rows, LANES), F32)] * 6,
        compiler_params=pltpu.CompilerParams(dimension_semantics=("arbitrary", "arbitrary"),
                                             vmem_limit_bytes=VMEM_LIMIT),
        name="rg_lru",
    )(xr, gate, cw, cb, wg, lam, *cast_stacks, *cast_gains)
    return outs[0], outs[1:]


def _bias_kernel(buckets_ref, table_ref, o_ref):
    buckets = buckets_ref[...]
    for h in range(N_HEADS):
        acc = jnp.zeros(buckets.shape, F32)
        for b in range(N_BUCKETS):
            acc = jnp.where(buckets == b, table_ref[b, h] * LOG2E, acc)
        g, p, e = h // KV_GROUP, (h % KV_GROUP) // 2, h % 2
        o_ref[2 * g + e, p * BLOCK:(p + 1) * BLOCK, :] = acc


def _rel_bias_table(buckets, rel_bias):
    return pl.pallas_call(
        _bias_kernel,
        in_specs=[pl.BlockSpec(memory_space=pltpu.VMEM), pl.BlockSpec(memory_space=pltpu.SMEM)],
        out_specs=pl.BlockSpec(memory_space=pltpu.VMEM),
        out_shape=jax.ShapeDtypeStruct((N_HEADS // 2, 2 * BLOCK, 3 * BLOCK), F32),
        name="rel_bias",
    )(buckets, rel_bias)


ATT_SETUP_ROWS = 512
ATT_BLOCKS = 1
ATT_UNROLL = 4
HALF = LANES // 2


def _attn_kernel(q_ref, k_ref, v_ref, bias_ref, sink_ref, o_ref, kvar, vvar, *, seq, layer):
    nb = seq // BLOCK
    lo_half = lax.broadcasted_iota(jnp.int32, (ATT_SETUP_ROWS, LANES), 1) < HALF
    zeros_blk = jnp.zeros((BLOCK, LANES), BF16)
    for u in range(2 * N_KV_HEADS):
        for ref in (kvar, vvar):
            ref[u, 0:BLOCK, :] = zeros_blk
            ref[u, BLOCK + seq:2 * BLOCK + seq, :] = zeros_blk
    for r in range(seq // ATT_SETUP_ROWS):
        src = slice(r * ATT_SETUP_ROWS, (r + 1) * ATT_SETUP_ROWS)
        dst = slice(BLOCK + r * ATT_SETUP_ROWS, BLOCK + (r + 1) * ATT_SETUP_ROWS)
        kc = k_ref[src, :]
        vc = v_ref[src, :]
        kr = pltpu.roll(kc, HALF, 1)
        vr = pltpu.roll(vc, HALF, 1)
        zero = jnp.zeros_like(kc)
        one = jnp.ones_like(vc)
        kvar[0, dst, :] = jnp.where(lo_half, kc, zero)
        kvar[1, dst, :] = jnp.where(lo_half, zero, kr)
        kvar[2, dst, :] = jnp.where(lo_half, kr, zero)
        kvar[3, dst, :] = jnp.where(lo_half, zero, kc)
        vvar[0, dst, :] = jnp.where(lo_half, vc, one)
        vvar[1, dst, :] = jnp.where(lo_half, one, vr)
        vvar[2, dst, :] = jnp.where(lo_half, vr, one)
        vvar[3, dst, :] = jnp.where(lo_half, one, vc)

    assert BLOCK == WINDOW == LANES
    t_idx = lax.broadcasted_iota(jnp.int32, (2 * BLOCK, LANES), 0) % BLOCK
    lane_minus_t = lax.broadcasted_iota(jnp.int32, (2 * BLOCK, LANES), 1) - t_idx
    first_head = lax.broadcasted_iota(jnp.int32, (2 * BLOCK, 1), 0) < BLOCK
    lo_out = lax.broadcasted_iota(jnp.int32, (BLOCK, LANES), 1) < HALF
    units = [(g, e) for g in range(N_KV_HEADS) for e in range(2)]
    never = 2 * LANES

    def blocks(i, carry):
        ns = [i * ATT_BLOCKS + j for j in range(ATT_BLOCKS)]
        starts = [pl.multiple_of(n * BLOCK, BLOCK) for n in ns]
        work = [(j, g, e) for j in range(ATT_BLOCKS) for g, e in units]
        scores = []
        for j, g, e in work:
            qblk = q_ref[pl.ds(starts[j], BLOCK), (2 * g) * LANES:(2 * g + 2) * LANES]
            lhs = jnp.concatenate([qblk[:, :LANES], qblk[:, LANES:]], axis=0)
            kwin = kvar[2 * g + e, pl.ds(starts[j], 3 * BLOCK), :]
            scores.append(lax.dot_general(lhs, kwin, (((1,), (1,)), ((), ())),
                                          preferred_element_type=F32))
        probs, sink_mass = [], []
        for (j, g, e), s in zip(work, scores):
            u = 2 * g + e
            prev_min = jnp.where(ns[j] == 0, never, 0)
            next_max = jnp.where(ns[j] == nb - 1, -never, 0)
            s_prev = jnp.where(lane_minus_t >= prev_min, s[:, :BLOCK] + bias_ref[u, :, :BLOCK], NEG_INF)
            s_own = s[:, BLOCK:2 * BLOCK] + bias_ref[u, :, BLOCK:2 * BLOCK]
            s_next = jnp.where(lane_minus_t <= next_max, s[:, 2 * BLOCK:] + bias_ref[u, :, 2 * BLOCK:], NEG_INF)
            sink = jnp.where(first_head, sink_ref[layer, 4 * g + e], sink_ref[layer, 4 * g + 2 + e]) * LOG2E
            row_max = jnp.max(jnp.maximum(jnp.maximum(s_prev, s_own), s_next), axis=-1, keepdims=True)
            m = jnp.maximum(row_max, sink)
            p = jnp.concatenate([jnp.exp2(s_prev - m), jnp.exp2(s_own - m), jnp.exp2(s_next - m)], axis=-1)
            probs.append(p.astype(BF16))
            sink_mass.append(jnp.exp2(sink - m))
        outs = []
        for (j, g, e), p in zip(work, probs):
            vwin = vvar[2 * g + e, pl.ds(starts[j], 3 * BLOCK), :]
            outs.append(jnp.dot(p, vwin, preferred_element_type=F32))
        for j in range(ATT_BLOCKS):
            chunks = []
            for g in range(N_KV_HEADS):
                for p in range(2):
                    rows = slice(p * BLOCK, (p + 1) * BLOCK)
                    ue, uo = j * len(units) + 2 * g, j * len(units) + 2 * g + 1
                    even, odd = outs[ue][rows], outs[uo][rows]
                    num = jnp.where(lo_out, even, odd)
                    den = (pltpu.roll(jnp.where(lo_out, odd, even), HALF, 1)
                           + jnp.where(lo_out, sink_mass[ue][rows], sink_mass[uo][rows]))
                    chunks.append(num / den)
            y = jnp.concatenate(chunks, axis=-1)
            o_ref[pl.ds(starts[j], BLOCK), :] = (y * _inv_rms(y)).astype(BF16)
        return carry

    lax.fori_loop(0, nb // ATT_BLOCKS, blocks, 0, unroll=ATT_UNROLL)


def _attn(l, q, k, v, bias, sink, batch, seq):
    assert seq % ATT_SETUP_ROWS == 0

    def seq_rows(width):
        return pl.BlockSpec((seq, width), lambda b: (b, 0))

    return pl.pallas_call(
        functools.partial(_attn_kernel, seq=seq, layer=l),
        grid=(batch,),
        in_specs=[seq_rows(ATT_WIDTH), seq_rows(KV_WIDTH), seq_rows(KV_WIDTH),
                  pl.BlockSpec(bias.shape, lambda b: (0, 0, 0), pipeline_mode=pl.Buffered(1)),
                  pl.BlockSpec(memory_space=pltpu.SMEM)],
        out_specs=seq_rows(ATT_WIDTH),
        out_shape=jax.ShapeDtypeStruct((batch * seq, ATT_WIDTH), BF16),
        scratch_shapes=[pltpu.VMEM((2 * N_KV_HEADS, seq + 2 * BLOCK, LANES), BF16)] * 2,
        compiler_params=pltpu.CompilerParams(dimension_semantics=("arbitrary",),
                                             vmem_limit_bytes=VMEM_LIMIT),
        name="windowed_gqa",
    )(q, k, v, bias, sink)


def _t5_bucket_index():
    t = jnp.arange(BLOCK)[:, None]
    j = jnp.arange(3 * BLOCK)[None, :]
    rel = j - BLOCK - t
    half = N_BUCKETS // 2
    max_exact = half // 2
    ret = (rel > 0).astype(jnp.int32) * half
    n = jnp.abs(rel)
    n_f = jnp.maximum(n, 1).astype(jnp.float32)
    large = max_exact + (jnp.log(n_f / max_exact) / math.log(MAX_DISTANCE / max_exact)
                         * (half - max_exact)).astype(jnp.int32)
    large = jnp.minimum(large, half - 1)
    return ret + jnp.where(n < max_exact, n, large)


def _gate_weights(w_a, b_a, w_x, b_x):
    depth = w_a.shape[0]
    n_slabs = LRU_WIDTH // LANES
    per_slab = LANES // LRU_BLOCK_W

    def slab_blockdiag(w):
        w = w.reshape(depth, n_slabs, per_slab, LRU_BLOCK_W, LRU_BLOCK_W)
        eye = jnp.eye(per_slab, dtype=w.dtype)
        full = jnp.einsum('lspcd,pq->lspcqd', w, eye)
        return full.reshape(depth, n_slabs, LANES, LANES)

    mats = [slab_blockdiag(w[:, d]) for d in range(2) for w in (w_a, w_x)]
    wg = (0.5 * jnp.concatenate(mats, axis=-1)).astype(BF16)
    biases = [b[:, d].reshape(depth, n_slabs, 1, LANES) for d in range(2) for b in (b_a, b_x)]
    bg = 0.5 * jnp.concatenate(biases, axis=-1)
    bg_hi = bg.astype(BF16)
    bg_lo = (bg - bg_hi.astype(F32)).astype(BF16)
    unused = jnp.zeros((depth, n_slabs, LANES - GATE_BIAS_ROWS, 4 * LANES), BF16)
    return jnp.concatenate([wg, bg_hi, bg_lo, unused], axis=2)


def kernel(x, ffn1_norm, ffn1_w_gate, ffn1_w_up, ffn1_w_down, mix_norm, w_in, conv_w, conv_b, lru_w_a, lru_b_a, lru_w_x, lru_b_x, lru_lambda, attn_sink, rel_bias, lru_out_norm, attn_out_norm, w_out, ffn2_norm, ffn2_w_gate, ffn2_w_up, ffn2_w_down, final_norm):
    batch, seq, d_model = x.shape
    depth = ffn1_norm.shape[0]
    tokens = batch * seq
    assert d_model == D_MODEL and seq % BLOCK == 0 and tokens % FFN1_TILE == 0 and tokens % FFN2_TILE == 0
    xt = x.reshape(tokens, D_MODEL)
    bias = _rel_bias_table(_t5_bucket_index(), rel_bias)
    gate_w = _gate_weights(lru_w_a, lru_b_a, lru_w_x, lru_b_x)
    cb = conv_b.reshape(depth, 1, -1)
    nf = final_norm.reshape(1, -1)
    mix_out_norm = jnp.concatenate([lru_out_norm, attn_out_norm], axis=-1)
    g_gate1, g_ffn1, g_mix, g_gate2, g_ffn2, g_out = jnp.stack(
        [0.5 * ffn1_norm, ffn1_norm, mix_norm, 0.5 * ffn2_norm, ffn2_norm, mix_out_norm])[..., None]
    half_step = jnp.full((depth, D_FF, 1), FFN_RES, F32)
    ffn1_stacks = (ffn1_w_gate, ffn1_w_up, ffn1_w_down, w_in)
    ffn1_gains = (g_gate1, g_ffn1, half_step, g_mix)
    ffn2_stacks = (ffn2_w_gate, ffn2_w_up, ffn2_w_down, w_out)
    ffn2_gains = (g_gate2, g_ffn2, half_step, g_out)
    ffn1_w = tuple((w[0] * g[0]).astype(BF16) for w, g in zip(ffn1_stacks, ffn1_gains))
    for l in range(depth):
        (xt, xr, gate, q, k, v), ffn2_w = _ffn1_inproj(l, xt, *ffn1_w, ffn2_stacks, ffn2_gains)
        last = l == depth - 1
        y_rec, ffn1_w = _rec(l, xr, gate, conv_w, cb, gate_w, lru_lambda, batch, seq,
                             () if last else ffn1_stacks, () if last else ffn1_gains, l + 1)
        y_att = _attn(l, q, k, v, bias, attn_sink, batch, seq)
        xt = _outproj_ffn2(xt, y_rec, y_att, ffn2_w[3], *ffn2_w[:3], nf, final_norm=last)
    return xt.reshape(batch, seq, D_MODEL)
```

```python
import functools
import math

import jax
import jax.numpy as jnp
from jax import lax
from jax.experimental import pallas as pl
from jax.experimental.pallas import tpu as pltpu

F32 = jnp.float32
BF16 = jnp.bfloat16

D_MODEL = 1024
LRU_WIDTH = 512
LRU_BLOCK_W = 64
LRU_C = 8.0
CONV_WIDTH = 4
N_HEADS = 8
N_KV_HEADS = 2
KV_GROUP = N_HEADS // N_KV_HEADS
HEAD_DIM = 64
ATT_WIDTH = N_HEADS * HEAD_DIM
KV_WIDTH = N_KV_HEADS * HEAD_DIM
WINDOW = 128
BLOCK = 128
N_BUCKETS = 32
MAX_DISTANCE = 128
D_FF = 2816
FFN_RES = 0.5
EPS = 1e-6
NEG_INF = -1e30
D_IN = 2 * LRU_WIDTH + ATT_WIDTH + 2 * KV_WIDTH
LOG2E = math.log2(math.e)

LANES = 128
SUBLANES = 8
FF_CHUNK = 256
N_FF_CHUNKS = D_FF // FF_CHUNK
FFN1_TILE = 512
FFN2_TILE = 1024
VMEM_LIMIT = 56 * 1024 * 1024

SCAN_SEGS = 32
SCAN_VREGS = SCAN_SEGS // SUBLANES
SEG_PITCH = 129
REC_CHUNK = 256
GATE_BIAS_ROWS = 2


def _rms(x, g):
    return x * lax.rsqrt(jnp.mean(x * x, axis=-1, keepdims=True) + EPS) * g


def _silu_of_twice(half):
    return half + half * jnp.tanh(half)


def _inv_rms(x):
    return lax.rsqrt(jnp.mean(x * x, axis=-1, keepdims=True) + EPS)


def _swiglu_into(xn_ref, wg_ref, wu_ref, wd_ref, acc_ref, residual):
    for c in range(N_FF_CHUNKS):
        cols = slice(c * FF_CHUNK, (c + 1) * FF_CHUNK)
        half_g = jnp.dot(xn_ref[...], wg_ref[:, cols], preferred_element_type=F32)
        u = jnp.dot(xn_ref[...], wu_ref[:, cols], preferred_element_type=F32)
        act = (_silu_of_twice(half_g) * u).astype(BF16)
        part = jnp.dot(act, wd_ref[cols, :], preferred_element_type=F32)
        if c == 0:
            acc_ref[...] = residual + part
        else:
            acc_ref[...] += part


N_CAST = 4


def _cast_blocks(src_refs, dst_refs):
    n = len(dst_refs)
    for w, gain, dst in zip(src_refs[:n], src_refs[n:], dst_refs):
        dst[...] = (w[...] * gain[...]).astype(BF16)


def _ffn1_inproj_kernel(x_ref, wg_ref, wu_ref, wd_ref, win_ref, *rest):
    cast_src, rest = rest[:2 * N_CAST], rest[2 * N_CAST:]
    xo_ref, xr_ref, gate_ref, q_ref, k_ref, v_ref = rest[:6]
    cast_dst, (xb_ref, acc_ref) = rest[6:6 + N_CAST], rest[6 + N_CAST:]
    _cast_blocks(cast_src, cast_dst)
    x = x_ref[...]
    xb_ref[...] = (x * _inv_rms(x)).astype(BF16)
    _swiglu_into(xb_ref, wg_ref, wu_ref, wd_ref, acc_ref, x)
    x1 = acc_ref[...]
    xo_ref[...] = x1
    xb_ref[...] = x1.astype(BF16)
    inv1 = _inv_rms(x1)

    def proj(rows, lo, width):
        return inv1[rows] * jnp.dot(xb_ref[rows, :], win_ref[:, lo:lo + width], preferred_element_type=F32)

    every = slice(None)
    tile = 2 * LANES
    for lo in range(0, LRU_WIDTH, tile):
        gate_ref[:, lo:lo + tile] = jax.nn.gelu(proj(every, LRU_WIDTH + lo, tile))
        xr_ref[:, lo:lo + tile] = proj(every, lo, tile)
    o = 2 * LRU_WIDTH
    q_ref[...] = (proj(every, o, ATT_WIDTH) * (HEAD_DIM ** -0.5 * LOG2E)).astype(BF16)
    o += ATT_WIDTH
    half = x_ref.shape[0] // 2
    for rows in (slice(0, half), slice(half, 2 * half)):
        kv = proj(rows, o, 2 * KV_WIDTH)
        k_ref[rows, :] = kv[:, :KV_WIDTH].astype(BF16)
        v_ref[rows, :] = kv[:, KV_WIDTH:].astype(BF16)


def _outproj_ffn2_kernel(x_ref, yr_ref, ya_ref, wo_ref, wg_ref, wu_ref, wd_ref, nf_ref,
                         xo_ref, xb_ref, acc_ref, *, final_norm):
    yr = yr_ref[...]
    y = jnp.concatenate([(yr * _inv_rms(yr)).astype(BF16), ya_ref[...]], axis=1)
    x1 = x_ref[...] + jnp.dot(y, wo_ref[...], preferred_element_type=F32)
    xb_ref[...] = (x1 * _inv_rms(x1)).astype(BF16)
    _swiglu_into(xb_ref, wg_ref, wu_ref, wd_ref, acc_ref, x1)
    x2 = acc_ref[...]
    if final_norm:
        x2 = _rms(x2, nf_ref[...])
    xo_ref[...] = x2


def _whole(arr):
    assert arr.ndim == 2
    return pl.BlockSpec(arr.shape, lambda *_: (0, 0), pipeline_mode=pl.Buffered(1))


def _rows(width, tm):
    return pl.BlockSpec((tm, width), lambda i: (i, 0))


def _cast_plan(stacks, gains, l, n_steps, step_of):
    bf16_rows = 2 * SUBLANES
    w_specs, g_specs, out_specs, out_shapes = [], [], [], []
    for w, gain in zip(stacks, gains):
        _, r, c = w.shape
        assert gain.shape == (w.shape[0], r, 1)
        n_blocks = n_steps
        while r % n_blocks or (r // n_blocks) % bf16_rows:
            n_blocks //= 2
        assert n_blocks >= 1 and n_steps % n_blocks == 0
        rep, rows = n_steps // n_blocks, r // n_blocks
        w_specs.append(pl.BlockSpec((None, rows, c), lambda *g, rep=rep: (l, step_of(*g) // rep, 0)))
        g_specs.append(pl.BlockSpec((None, rows, 1), lambda *g, rep=rep: (l, step_of(*g) // rep, 0)))
        out_specs.append(pl.BlockSpec((rows, c), lambda *g, rep=rep: (step_of(*g) // rep, 0)))
        out_shapes.append(jax.ShapeDtypeStruct((r, c), BF16))
    return w_specs + g_specs, out_specs, out_shapes


def _ffn1_inproj(l, x, wg, wu, wd, win, cast_stacks, cast_gains):
    t = x.shape[0]
    tm = FFN1_TILE
    out_shape = [
        jax.ShapeDtypeStruct((t, D_MODEL), F32),
        jax.ShapeDtypeStruct((t, LRU_WIDTH), F32),
        jax.ShapeDtypeStruct((t, LRU_WIDTH), F32),
        jax.ShapeDtypeStruct((t, ATT_WIDTH), BF16),
        jax.ShapeDtypeStruct((t, KV_WIDTH), BF16),
        jax.ShapeDtypeStruct((t, KV_WIDTH), BF16),
    ]
    out_specs = [_rows(D_MODEL, tm), _rows(LRU_WIDTH, tm), _rows(LRU_WIDTH, tm), _rows(ATT_WIDTH, tm),
                 _rows(KV_WIDTH, tm), _rows(KV_WIDTH, tm)]
    cast_in, cast_out, cast_shapes = _cast_plan(cast_stacks, cast_gains, l, t // tm, lambda i: i)
    outs = pl.pallas_call(
        _ffn1_inproj_kernel,
        grid=(t // tm,),
        in_specs=[_rows(D_MODEL, tm), _whole(wg), _whole(wu), _whole(wd), _whole(win)] + cast_in,
        out_specs=out_specs + cast_out,
        out_shape=out_shape + cast_shapes,
        scratch_shapes=[pltpu.VMEM((tm, D_MODEL), BF16), pltpu.VMEM((tm, D_MODEL), F32)],
        compiler_params=pltpu.CompilerParams(dimension_semantics=("arbitrary",),
                                             vmem_limit_bytes=VMEM_LIMIT),
        name="ffn1_inproj",
    )(x, wg, wu, wd, win, *cast_stacks, *cast_gains)
    return outs[:6], outs[6:]


def _outproj_ffn2(x, yr, ya, wo, wg, wu, wd, nf, final_norm):
    t = x.shape[0]
    tm = FFN2_TILE
    return pl.pallas_call(
        functools.partial(_outproj_ffn2_kernel, final_norm=final_norm),
        grid=(t // tm,),
        in_specs=[_rows(D_MODEL, tm), _rows(LRU_WIDTH, tm), _rows(ATT_WIDTH, tm), _whole(wo),
                  _whole(wg), _whole(wu), _whole(wd), _whole(nf)],
        out_specs=_rows(D_MODEL, tm),
        out_shape=jax.ShapeDtypeStruct((t, D_MODEL), F32),
        scratch_shapes=[pltpu.VMEM((tm, D_MODEL), BF16), pltpu.VMEM((tm, D_MODEL), F32)],
        compiler_params=pltpu.CompilerParams(dimension_semantics=("arbitrary",),
                                             vmem_limit_bytes=VMEM_LIMIT),
        name="outproj_ffn2",
    )(x, yr, ya, wo, wg, wu, wd, nf)


def _rec_kernel(xr_ref, gate_ref, cw_ref, cb_ref, wg_ref, lam_ref, *rest, seq, n_cast):
    cast_src, o_ref = rest[:2 * n_cast], rest[2 * n_cast]
    cast_dst = rest[2 * n_cast + 1:3 * n_cast + 1]
    a_f, u_f, a_b, u_b, h_f, h_b = rest[3 * n_cast + 1:]
    _cast_blocks(cast_src, cast_dst)
    n_chunks = seq // REC_CHUNK
    n_steps = SEG_PITCH
    pad_rows = SCAN_SEGS * SEG_PITCH - seq
    chunk_row = lax.broadcasted_iota(jnp.int32, (REC_CHUNK, LANES), 0)

    def shifted_rows(c, offset):
        lo = c * REC_CHUNK + offset
        if lo >= 0 and lo + REC_CHUNK <= seq:
            return xr_ref[lo:lo + REC_CHUNK, :]
        rolled = pltpu.roll(xr_ref[c * REC_CHUNK:(c + 1) * REC_CHUNK, :], (-offset) % REC_CHUNK, 0)
        outside = (chunk_row < -offset) if offset < 0 else (chunk_row >= REC_CHUNK - offset)
        return jnp.where(outside, 0.0, rolled)

    for a_ref, u_ref in ((a_f, u_f), (a_b, u_b)):
        a_ref[seq:seq + pad_rows, :] = jnp.ones((pad_rows, LANES), F32)
        u_ref[seq:seq + pad_rows, :] = jnp.zeros((pad_rows, LANES), F32)

    lam = lam_ref[...]
    neg_softplus = -(jnp.maximum(-lam, 0.0) + jnp.log1p(jnp.exp(-jnp.abs(lam))))
    half_log2_a = (0.5 * LRU_C * LOG2E) * neg_softplus

    cw = cw_ref[...]
    cb = cb_ref[...]
    bias_cols = (lax.broadcasted_iota(jnp.int32, (REC_CHUNK, LANES), 1) < GATE_BIAS_ROWS).astype(BF16)
    for c in range(n_chunks):
        xc = cb
        for tap in range(CONV_WIDTH):
            xc = xc + cw[tap:tap + 1, :] * shifted_rows(c, tap - CONV_WIDTH // 2)
        lhs = jnp.concatenate([xc.astype(BF16), bias_cols], axis=1)
        th = jnp.tanh(jnp.dot(lhs, wg_ref[...], preferred_element_type=F32))
        half_xc = 0.5 * xc
        rows = slice(c * REC_CHUNK, (c + 1) * REC_CHUNK)
        for d, (a_ref, u_ref) in enumerate(((a_f, u_f), (a_b, u_b))):
            th_r = th[:, (2 * d) * LANES:(2 * d + 1) * LANES]
            th_i = th[:, (2 * d + 1) * LANES:(2 * d + 2) * LANES]
            scale = half_log2_a[d:d + 1, :]
            a = jnp.exp2(scale * th_r + scale)
            v = 1.0 - a * a
            root = jnp.where(v > 0.0, v * lax.rsqrt(v), 0.0)
            a_ref[rows, :] = a
            u_ref[rows, :] = root * (th_i * half_xc + half_xc)

    def seg_rows(j):
        return pl.ds(j, SCAN_SEGS, stride=SEG_PITCH)

    zeros = jnp.zeros((SCAN_SEGS, LANES), F32)
    ones = jnp.ones((SCAN_SEGS, LANES), F32)

    def segment_maps(j, carry):
        hf, pf, hb, pb = carry
        jb = n_steps - 1 - j
        af = a_f[seg_rows(j), :]
        ab = a_b[seg_rows(jb), :]
        hf = af * hf + u_f[seg_rows(j), :]
        hb = ab * hb + u_b[seg_rows(jb), :]
        return hf, pf * af, hb, pb * ab

    hf, pf, hb, pb = lax.fori_loop(0, n_steps, segment_maps, (zeros, ones, zeros, ones), unroll=4)

    seg = lax.broadcasted_iota(jnp.int32, (SCAN_SEGS, LANES), 0)

    def entering(h, p, toward_higher):
        d = 1
        while d < SCAN_SEGS:
            shift = d if toward_higher else SCAN_SEGS - d
            have = (seg >= d) if toward_higher else (seg < SCAN_SEGS - d)
            h = jnp.where(have, h + p * pltpu.roll(h, shift, 0), h)
            p = jnp.where(have, p * pltpu.roll(p, shift, 0), p)
            d *= 2
        one_step = 1 if toward_higher else SCAN_SEGS - 1
        edge = (seg == 0) if toward_higher else (seg == SCAN_SEGS - 1)
        return jnp.where(edge, 0.0, pltpu.roll(h, one_step, 0))

    hf0 = entering(hf, pf, True)
    hb0 = entering(hb, pb, False)

    def true_states(j, carry):
        hf, hb = carry
        jb = n_steps - 1 - j
        hf = a_f[seg_rows(j), :] * hf + u_f[seg_rows(j), :]
        hb = a_b[seg_rows(jb), :] * hb + u_b[seg_rows(jb), :]
        h_f[seg_rows(j), :] = hf
        h_b[seg_rows(jb), :] = hb
        return hf, hb

    lax.fori_loop(0, n_steps, true_states, (hf0, hb0), unroll=4)

    for c in range(n_chunks):
        rows = slice(c * REC_CHUNK, (c + 1) * REC_CHUNK)
        o_ref[rows, :] = gate_ref[rows, :] * (h_f[rows, :] + h_b[rows, :])


def _rec(l, xr, gate, cw, cb, wg, lam, batch, seq, cast_stacks, cast_gains, cast_layer):
    n_slabs = LRU_WIDTH // LANES
    assert SCAN_SEGS * SEG_PITCH >= seq > (SCAN_SEGS - 1) * SEG_PITCH and seq % REC_CHUNK == 0
    slab = pl.BlockSpec((seq, LANES), lambda b, s: (b, s))
    scan_rows = SCAN_SEGS * SEG_PITCH
    cast_in, cast_out, cast_shapes = _cast_plan(cast_stacks, cast_gains, cast_layer, batch * n_slabs,
                                                lambda b, s: b * n_slabs + s)
    outs = pl.pallas_call(
        functools.partial(_rec_kernel, seq=seq, n_cast=len(cast_stacks)),
        grid=(batch, n_slabs),
        in_specs=[slab, slab,
                  pl.BlockSpec((None, CONV_WIDTH, LANES), lambda b, s: (l, 0, s)),
                  pl.BlockSpec((None, 1, LANES), lambda b, s: (l, 0, s)),
                  pl.BlockSpec((None, None, 2 * LANES, 4 * LANES), lambda b, s: (l, s, 0, 0)),
                  pl.BlockSpec((None, 2, LANES), lambda b, s: (l, 0, s))] + cast_in,
        out_specs=[slab] + cast_out,
        out_shape=[jax.ShapeDtypeStruct((batch * seq, LRU_WIDTH), F32)] + cast_shapes,
        scratch_shapes=[pltpu.VMEM((scan_rows, LANES), F32)] * 6,
        compiler_params=pltpu.CompilerParams(dimension_semantics=("arbitrary", "arbitrary"),
                                             vmem_limit_bytes=VMEM_LIMIT),
        name="rg_lru",
    )(xr, gate, cw, cb, wg, lam, *cast_stacks, *cast_gains)
    return outs[0], outs[1:]


def _bias_kernel(buckets_ref, table_ref, o_ref):
    buckets = buckets_ref[...]
    for h in range(N_HEADS):
        acc = jnp.zeros(buckets.shape, F32)
        for b in range(N_BUCKETS):
            acc = jnp.where(buckets == b, table_ref[b, h] * LOG2E, acc)
        g, p, e = h // KV_GROUP, (h % KV_GROUP) // 2, h % 2
        o_ref[2 * g + e, p * BLOCK:(p + 1) * BLOCK, :] = acc


def _rel_bias_table(buckets, rel_bias):
    return pl.pallas_call(
        _bias_kernel,
        in_specs=[pl.BlockSpec(memory_space=pltpu.VMEM), pl.BlockSpec(memory_space=pltpu.SMEM)],
        out_specs=pl.BlockSpec(memory_space=pltpu.VMEM),
        out_shape=jax.ShapeDtypeStruct((N_HEADS // 2, 2 * BLOCK, 3 * BLOCK), F32),
        name="rel_bias",
    )(buckets, rel_bias)


ATT_SETUP_ROWS = 512
ATT_BLOCKS = 1
ATT_UNROLL = 2
HALF = LANES // 2


def _attn_kernel(q_ref, k_ref, v_ref, bias_ref, sink_ref, o_ref, kvar, vvar, *, seq, layer):
    nb = seq // BLOCK
    lo_half = lax.broadcasted_iota(jnp.int32, (ATT_SETUP_ROWS, LANES), 1) < HALF
    zeros_blk = jnp.zeros((BLOCK, LANES), BF16)
    for u in range(2 * N_KV_HEADS):
        for ref in (kvar, vvar):
            ref[u, 0:BLOCK, :] = zeros_blk
            ref[u, BLOCK + seq:2 * BLOCK + seq, :] = zeros_blk
    for r in range(seq // ATT_SETUP_ROWS):
        src = slice(r * ATT_SETUP_ROWS, (r + 1) * ATT_SETUP_ROWS)
        dst = slice(BLOCK + r * ATT_SETUP_ROWS, BLOCK + (r + 1) * ATT_SETUP_ROWS)
        kc = k_ref[src, :]
        vc = v_ref[src, :]
        kr = pltpu.roll(kc, HALF, 1)
        vr = pltpu.roll(vc, HALF, 1)
        zero = jnp.zeros_like(kc)
        one = jnp.ones_like(vc)
        kvar[0, dst, :] = jnp.where(lo_half, kc, zero)
        kvar[1, dst, :] = jnp.where(lo_half, zero, kr)
        kvar[2, dst, :] = jnp.where(lo_half, kr, zero)
        kvar[3, dst, :] = jnp.where(lo_half, zero, kc)
        vvar[0, dst, :] = jnp.where(lo_half, vc, one)
        vvar[1, dst, :] = jnp.where(lo_half, one, vr)
        vvar[2, dst, :] = jnp.where(lo_half, vr, one)
        vvar[3, dst, :] = jnp.where(lo_half, one, vc)

    assert BLOCK == WINDOW == LANES
    t_idx = lax.broadcasted_iota(jnp.int32, (2 * BLOCK, LANES), 0) % BLOCK
    lane_minus_t = lax.broadcasted_iota(jnp.int32, (2 * BLOCK, LANES), 1) - t_idx
    first_head = lax.broadcasted_iota(jnp.int32, (2 * BLOCK, 1), 0) < BLOCK
    lo_out = lax.broadcasted_iota(jnp.int32, (BLOCK, LANES), 1) < HALF
    units = [(g, e) for g in range(N_KV_HEADS) for e in range(2)]
    never = 2 * LANES

    def blocks(i, carry):
        ns = [i * ATT_BLOCKS + j for j in range(ATT_BLOCKS)]
        starts = [pl.multiple_of(n * BLOCK, BLOCK) for n in ns]
        work = [(j, g, e) for j in range(ATT_BLOCKS) for g, e in units]
        scores = []
        for j, g, e in work:
            qblk = q_ref[pl.ds(starts[j], BLOCK), (2 * g) * LANES:(2 * g + 2) * LANES]
            lhs = jnp.concatenate([qblk[:, :LANES], qblk[:, LANES:]], axis=0)
            kwin = kvar[2 * g + e, pl.ds(starts[j], 3 * BLOCK), :]
            scores.append(lax.dot_general(lhs, kwin, (((1,), (1,)), ((), ())),
                                          preferred_element_type=F32))
        probs, sink_mass = [], []
        for (j, g, e), s in zip(work, scores):
            u = 2 * g + e
            prev_min = jnp.where(ns[j] == 0, never, 0)
            next_max = jnp.where(ns[j] == nb - 1, -never, 0)
            s_prev = jnp.where(lane_minus_t >= prev_min, s[:, :BLOCK] + bias_ref[u, :, :BLOCK], NEG_INF)
            s_own = s[:, BLOCK:2 * BLOCK] + bias_ref[u, :, BLOCK:2 * BLOCK]
            s_next = jnp.where(lane_minus_t <= next_max, s[:, 2 * BLOCK:] + bias_ref[u, :, 2 * BLOCK:], NEG_INF)
            sink = jnp.where(first_head, sink_ref[layer, 4 * g + e], sink_ref[layer, 4 * g + 2 + e]) * LOG2E
            row_max = jnp.max(jnp.maximum(jnp.maximum(s_prev, s_own), s_next), axis=-1, keepdims=True)
            m = jnp.maximum(row_max, sink)
            p = jnp.concatenate([jnp.exp2(s_prev - m), jnp.exp2(s_own - m), jnp.exp2(s_next - m)], axis=-1)
            probs.append(p.astype(BF16))
            sink_mass.append(jnp.exp2(sink - m))
        outs = []
        for (j, g, e), p in zip(work, probs):
            vwin = vvar[2 * g + e, pl.ds(starts[j], 3 * BLOCK), :]
            outs.append(jnp.dot(p, vwin, preferred_element_type=F32))
        for j in range(ATT_BLOCKS):
            chunks = []
            for g in range(N_KV_HEADS):
                for p in range(2):
                    rows = slice(p * BLOCK, (p + 1) * BLOCK)
                    ue, uo = j * len(units) + 2 * g, j * len(units) + 2 * g + 1
                    even, odd = outs[ue][rows], outs[uo][rows]
                    num = jnp.where(lo_out, even, odd)
                    den = (pltpu.roll(jnp.where(lo_out, odd, even), HALF, 1)
                           + jnp.where(lo_out, sink_mass[ue][rows], sink_mass[uo][rows]))
                    chunks.append(num / den)
            y = jnp.concatenate(chunks, axis=-1)
            o_ref[pl.ds(starts[j], BLOCK), :] = (y * _inv_rms(y)).astype(BF16)
        return carry

    lax.fori_loop(0, nb // ATT_BLOCKS, blocks, 0, unroll=ATT_UNROLL)


def _attn(l, q, k, v, bias, sink, batch, seq):
    assert seq % ATT_SETUP_ROWS == 0

    def seq_rows(width):
        return pl.BlockSpec((seq, width), lambda b: (b, 0))

    return pl.pallas_call(
        functools.partial(_attn_kernel, seq=seq, layer=l),
        grid=(batch,),
        in_specs=[seq_rows(ATT_WIDTH), seq_rows(KV_WIDTH), seq_rows(KV_WIDTH),
                  pl.BlockSpec(bias.shape, lambda b: (0, 0, 0), pipeline_mode=pl.Buffered(1)),
                  pl.BlockSpec(memory_space=pltpu.SMEM)],
        out_specs=seq_rows(ATT_WIDTH),
        out_shape=jax.ShapeDtypeStruct((batch * seq, ATT_WIDTH), BF16),
        scratch_shapes=[pltpu.VMEM((2 * N_KV_HEADS, seq + 2 * BLOCK, LANES), BF16)] * 2,
        compiler_params=pltpu.CompilerParams(dimension_semantics=("arbitrary",),
                                             vmem_limit_bytes=VMEM_LIMIT),
        name="windowed_gqa",
    )(q, k, v, bias, sink)


def _t5_bucket_index():
    t = jnp.arange(BLOCK)[:, None]
    j = jnp.arange(3 * BLOCK)[None, :]
    rel = j - BLOCK - t
    half = N_BUCKETS // 2
    max_exact = half // 2
    ret = (rel > 0).astype(jnp.int32) * half
    n = jnp.abs(rel)
    n_f = jnp.maximum(n, 1).astype(jnp.float32)
    large = max_exact + (jnp.log(n_f / max_exact) / math.log(MAX_DISTANCE / max_exact)
                         * (half - max_exact)).astype(jnp.int32)
    large = jnp.minimum(large, half - 1)
    return ret + jnp.where(n < max_exact, n, large)


def _gate_weights(w_a, b_a, w_x, b_x):
    depth = w_a.shape[0]
    n_slabs = LRU_WIDTH // LANES
    per_slab = LANES // LRU_BLOCK_W

    def slab_blockdiag(w):
        w = w.reshape(depth, n_slabs, per_slab, LRU_BLOCK_W, LRU_BLOCK_W)
        eye = jnp.eye(per_slab, dtype=w.dtype)
        full = jnp.einsum('lspcd,pq->lspcqd', w, eye)
        return full.reshape(depth, n_slabs, LANES, LANES)

    mats = [slab_blockdiag(w[:, d]) for d in range(2) for w in (w_a, w_x)]
    wg = (0.5 * jnp.concatenate(mats, axis=-1)).astype(BF16)
    biases = [b[:, d].reshape(depth, n_slabs, 1, LANES) for d in range(2) for b in (b_a, b_x)]
    bg = 0.5 * jnp.concatenate(biases, axis=-1)
    bg_hi = bg.astype(BF16)
    bg_lo = (bg - bg_hi.astype(F32)).astype(BF16)
    unused = jnp.zeros((depth, n_slabs, LANES - GATE_BIAS_ROWS, 4 * LANES), BF16)
    return jnp.concatenate([wg, bg_hi, bg_lo, unused], axis=2)


def kernel(x, ffn1_norm, ffn1_w_gate, ffn1_w_up, ffn1_w_down, mix_norm, w_in, conv_w, conv_b, lru_w_a, lru_b_a, lru_w_x, lru_b_x, lru_lambda, attn_sink, rel_bias, lru_out_norm, attn_out_norm, w_out, ffn2_norm, ffn2_w_gate, ffn2_w_up, ffn2_w_down, final_norm):
    batch, seq, d_model = x.shape
    depth = ffn1_norm.shape[0]
    tokens = batch * seq
    assert d_model == D_MODEL and seq % BLOCK == 0 and tokens % FFN1_TILE == 0 and tokens % FFN2_TILE == 0
    xt = x.reshape(tokens, D_MODEL)
    bias = _rel_bias_table(_t5_bucket_index(), rel_bias)
    gate_w = _gate_weights(lru_w_a, lru_b_a, lru_w_x, lru_b_x)
    cb = conv_b.reshape(depth, 1, -1)
    nf = final_norm.reshape(1, -1)
    mix_out_norm = jnp.concatenate([lru_out_norm, attn_out_norm], axis=-1)
    g_gate1, g_ffn1, g_mix, g_gate2, g_ffn2, g_out = jnp.stack(
        [0.5 * ffn1_norm, ffn1_norm, mix_norm, 0.5 * ffn2_norm, ffn2_norm, mix_out_norm])[..., None]
    half_step = jnp.full((depth, D_FF, 1), FFN_RES, F32)
    ffn1_stacks = (ffn1_w_gate, ffn1_w_up, ffn1_w_down, w_in)
    ffn1_gains = (g_gate1, g_ffn1, half_step, g_mix)
    ffn2_stacks = (ffn2_w_gate, ffn2_w_up, ffn2_w_down, w_out)
    ffn2_gains = (g_gate2, g_ffn2, half_step, g_out)
    ffn1_w = tuple((w[0] * g[0]).astype(BF16) for w, g in zip(ffn1_stacks, ffn1_gains))
    for l in range(depth):
        (xt, xr, gate, q, k, v), ffn2_w = _ffn1_inproj(l, xt, *ffn1_w, ffn2_stacks, ffn2_gains)
        last = l == depth - 1
        y_rec, ffn1_w = _rec(l, xr, gate, conv_w, cb, gate_w, lru_lambda, batch, seq,
                             () if last else ffn1_stacks, () if last else ffn1_gains, l + 1)
        y_att = _attn(l, q, k, v, bias, attn_sink, batch, seq)
        xt = _outproj_ffn2(xt, y_rec, y_att, ffn2_w[3], *ffn2_w[:3], nf, final_norm=last)
    return xt.reshape(batch, seq, D_MODEL)
```

```python
import functools
import math

import jax
import jax.numpy as jnp
from jax import lax
from jax.experimental import pallas as pl
from jax.experimental.pallas import tpu as pltpu

F32 = jnp.float32
BF16 = jnp.bfloat16

D_MODEL = 1024
LRU_WIDTH = 512
LRU_BLOCK_W = 64
LRU_C = 8.0
CONV_WIDTH = 4
N_HEADS = 8
N_KV_HEADS = 2
KV_GROUP = N_HEADS // N_KV_HEADS
HEAD_DIM = 64
ATT_WIDTH = N_HEADS * HEAD_DIM
KV_WIDTH = N_KV_HEADS * HEAD_DIM
WINDOW = 128
BLOCK = 128
N_BUCKETS = 32
MAX_DISTANCE = 128
D_FF = 2816
FFN_RES = 0.5
EPS = 1e-6
NEG_INF = -1e30
D_IN = 2 * LRU_WIDTH + ATT_WIDTH + 2 * KV_WIDTH
LOG2E = math.log2(math.e)

LANES = 128
SUBLANES = 8
FF_CHUNK = 256
N_FF_CHUNKS = D_FF // FF_CHUNK
FFN1_TILE = 512
FFN2_TILE = 1024
VMEM_LIMIT = 56 * 1024 * 1024

SCAN_SEGS = 32
SCAN_VREGS = SCAN_SEGS // SUBLANES
SEG_PITCH = 129
REC_CHUNK = 512
GATE_BIAS_ROWS = 2


def _rms(x, g):
    return x * lax.rsqrt(jnp.mean(x * x, axis=-1, keepdims=True) + EPS) * g


def _silu_of_twice(half):
    return half + half * jnp.tanh(half)


def _inv_rms(x):
    return lax.rsqrt(jnp.mean(x * x, axis=-1, keepdims=True) + EPS)


def _swiglu_into(xn_ref, wg_ref, wu_ref, wd_ref, acc_ref, residual):
    for c in range(N_FF_CHUNKS):
        cols = slice(c * FF_CHUNK, (c + 1) * FF_CHUNK)
        half_g = jnp.dot(xn_ref[...], wg_ref[:, cols], preferred_element_type=F32)
        u = jnp.dot(xn_ref[...], wu_ref[:, cols], preferred_element_type=F32)
        act = (_silu_of_twice(half_g) * u).astype(BF16)
        part = jnp.dot(act, wd_ref[cols, :], preferred_element_type=F32)
        if c == 0:
            acc_ref[...] = residual + part
        else:
            acc_ref[...] += part


N_CAST = 4


def _cast_blocks(src_refs, dst_refs):
    n = len(dst_refs)
    for w, gain, dst in zip(src_refs[:n], src_refs[n:], dst_refs):
        dst[...] = (w[...] * gain[...]).astype(BF16)


def _ffn1_inproj_kernel(x_ref, wg_ref, wu_ref, wd_ref, win_ref, *rest):
    cast_src, rest = rest[:2 * N_CAST], rest[2 * N_CAST:]
    xo_ref, xr_ref, gate_ref, q_ref, k_ref, v_ref = rest[:6]
    cast_dst, (xb_ref, acc_ref) = rest[6:6 + N_CAST], rest[6 + N_CAST:]
    _cast_blocks(cast_src, cast_dst)
    x = x_ref[...]
    xb_ref[...] = (x * _inv_rms(x)).astype(BF16)
    _swiglu_into(xb_ref, wg_ref, wu_ref, wd_ref, acc_ref, x)
    x1 = acc_ref[...]
    xo_ref[...] = x1
    xb_ref[...] = x1.astype(BF16)
    inv1 = _inv_rms(x1)

    def proj(rows, lo, width):
        return inv1[rows] * jnp.dot(xb_ref[rows, :], win_ref[:, lo:lo + width], preferred_element_type=F32)

    every = slice(None)
    tile = 2 * LANES
    for lo in range(0, LRU_WIDTH, tile):
        gate_ref[:, lo:lo + tile] = jax.nn.gelu(proj(every, LRU_WIDTH + lo, tile))
        xr_ref[:, lo:lo + tile] = proj(every, lo, tile)
    o = 2 * LRU_WIDTH
    q_ref[...] = (proj(every, o, ATT_WIDTH) * (HEAD_DIM ** -0.5 * LOG2E)).astype(BF16)
    o += ATT_WIDTH
    half = x_ref.shape[0] // 2
    for rows in (slice(0, half), slice(half, 2 * half)):
        kv = proj(rows, o, 2 * KV_WIDTH)
        k_ref[rows, :] = kv[:, :KV_WIDTH].astype(BF16)
        v_ref[rows, :] = kv[:, KV_WIDTH:].astype(BF16)


def _outproj_ffn2_kernel(x_ref, yr_ref, ya_ref, wo_ref, wg_ref, wu_ref, wd_ref, nf_ref,
                         xo_ref, xb_ref, acc_ref, *, final_norm):
    yr = yr_ref[...]
    y = jnp.concatenate([(yr * _inv_rms(yr)).astype(BF16), ya_ref[...]], axis=1)
    x1 = x_ref[...] + jnp.dot(y, wo_ref[...], preferred_element_type=F32)
    xb_ref[...] = (x1 * _inv_rms(x1)).astype(BF16)
    _swiglu_into(xb_ref, wg_ref, wu_ref, wd_ref, acc_ref, x1)
    x2 = acc_ref[...]
    if final_norm:
        x2 = _rms(x2, nf_ref[...])
    xo_ref[...] = x2


def _whole(arr):
    assert arr.ndim == 2
    return pl.BlockSpec(arr.shape, lambda *_: (0, 0), pipeline_mode=pl.Buffered(1))


def _rows(width, tm):
    return pl.BlockSpec((tm, width), lambda i: (i, 0))


def _cast_plan(stacks, gains, l, n_steps, step_of):
    bf16_rows = 2 * SUBLANES
    w_specs, g_specs, out_specs, out_shapes = [], [], [], []
    for w, gain in zip(stacks, gains):
        _, r, c = w.shape
        assert gain.shape == (w.shape[0], r, 1)
        n_blocks = n_steps
        while r % n_blocks or (r // n_blocks) % bf16_rows:
            n_blocks //= 2
        assert n_blocks >= 1 and n_steps % n_blocks == 0
        rep, rows = n_steps // n_blocks, r // n_blocks
        w_specs.append(pl.BlockSpec((None, rows, c), lambda *g, rep=rep: (l, step_of(*g) // rep, 0)))
        g_specs.append(pl.BlockSpec((None, rows, 1), lambda *g, rep=rep: (l, step_of(*g) // rep, 0)))
        out_specs.append(pl.BlockSpec((rows, c), lambda *g, rep=rep: (step_of(*g) // rep, 0)))
        out_shapes.append(jax.ShapeDtypeStruct((r, c), BF16))
    return w_specs + g_specs, out_specs, out_shapes


def _ffn1_inproj(l, x, wg, wu, wd, win, cast_stacks, cast_gains):
    t = x.shape[0]
    tm = FFN1_TILE
    out_shape = [
        jax.ShapeDtypeStruct((t, D_MODEL), F32),
        jax.ShapeDtypeStruct((t, LRU_WIDTH), F32),
        jax.ShapeDtypeStruct((t, LRU_WIDTH), F32),
        jax.ShapeDtypeStruct((t, ATT_WIDTH), BF16),
        jax.ShapeDtypeStruct((t, KV_WIDTH), BF16),
        jax.ShapeDtypeStruct((t, KV_WIDTH), BF16),
    ]
    out_specs = [_rows(D_MODEL, tm), _rows(LRU_WIDTH, tm), _rows(LRU_WIDTH, tm), _rows(ATT_WIDTH, tm),
                 _rows(KV_WIDTH, tm), _rows(KV_WIDTH, tm)]
    cast_in, cast_out, cast_shapes = _cast_plan(cast_stacks, cast_gains, l, t // tm, lambda i: i)
    outs = pl.pallas_call(
        _ffn1_inproj_kernel,
        grid=(t // tm,),
        in_specs=[_rows(D_MODEL, tm), _whole(wg), _whole(wu), _whole(wd), _whole(win)] + cast_in,
        out_specs=out_specs + cast_out,
        out_shape=out_shape + cast_shapes,
        scratch_shapes=[pltpu.VMEM((tm, D_MODEL), BF16), pltpu.VMEM((tm, D_MODEL), F32)],
        compiler_params=pltpu.CompilerParams(dimension_semantics=("arbitrary",),
                                             vmem_limit_bytes=VMEM_LIMIT),
        name="ffn1_inproj",
    )(x, wg, wu, wd, win, *cast_stacks, *cast_gains)
    return outs[:6], outs[6:]


def _outproj_ffn2(x, yr, ya, wo, wg, wu, wd, nf, final_norm):
    t = x.shape[0]
    tm = FFN2_TILE
    return pl.pallas_call(
        functools.partial(_outproj_ffn2_kernel, final_norm=final_norm),
        grid=(t // tm,),
        in_specs=[_rows(D_MODEL, tm), _rows(LRU_WIDTH, tm), _rows(ATT_WIDTH, tm), _whole(wo),
                  _whole(wg), _whole(wu), _whole(wd), _whole(nf)],
        out_specs=_rows(D_MODEL, tm),
        out_shape=jax.ShapeDtypeStruct((t, D_MODEL), F32),
        scratch_shapes=[pltpu.VMEM((tm, D_MODEL), BF16), pltpu.VMEM((tm, D_MODEL), F32)],
        compiler_params=pltpu.CompilerParams(dimension_semantics=("arbitrary",),
                                             vmem_limit_bytes=VMEM_LIMIT),
        name="outproj_ffn2",
    )(x, yr, ya, wo, wg, wu, wd, nf)


def _rec_kernel(xr_ref, gate_ref, cw_ref, cb_ref, wg_ref, lam_ref, *rest, seq, n_cast):
    cast_src, o_ref = rest[:2 * n_cast], rest[2 * n_cast]
    cast_dst = rest[2 * n_cast + 1:3 * n_cast + 1]
    a_f, u_f, a_b, u_b, h_f, h_b = rest[3 * n_cast + 1:]
    _cast_blocks(cast_src, cast_dst)
    n_chunks = seq // REC_CHUNK
    n_steps = SEG_PITCH
    pad_rows = SCAN_SEGS * SEG_PITCH - seq
    chunk_row = lax.broadcasted_iota(jnp.int32, (REC_CHUNK, LANES), 0)

    def shifted_rows(c, offset):
        lo = c * REC_CHUNK + offset
        if lo >= 0 and lo + REC_CHUNK <= seq:
            return xr_ref[lo:lo + REC_CHUNK, :]
        rolled = pltpu.roll(xr_ref[c * REC_CHUNK:(c + 1) * REC_CHUNK, :], (-offset) % REC_CHUNK, 0)
        outside = (chunk_row < -offset) if offset < 0 else (chunk_row >= REC_CHUNK - offset)
        return jnp.where(outside, 0.0, rolled)

    for a_ref, u_ref in ((a_f, u_f), (a_b, u_b)):
        a_ref[seq:seq + pad_rows, :] = jnp.ones((pad_rows, LANES), F32)
        u_ref[seq:seq + pad_rows, :] = jnp.zeros((pad_rows, LANES), F32)

    lam = lam_ref[...]
    neg_softplus = -(jnp.maximum(-lam, 0.0) + jnp.log1p(jnp.exp(-jnp.abs(lam))))
    half_log2_a = (0.5 * LRU_C * LOG2E) * neg_softplus

    cw = cw_ref[...]
    cb = cb_ref[...]
    bias_cols = (lax.broadcasted_iota(jnp.int32, (REC_CHUNK, LANES), 1) < GATE_BIAS_ROWS).astype(BF16)
    for c in range(n_chunks):
        xc = cb
        for tap in range(CONV_WIDTH):
            xc = xc + cw[tap:tap + 1, :] * shifted_rows(c, tap - CONV_WIDTH // 2)
        lhs = jnp.concatenate([xc.astype(BF16), bias_cols], axis=1)
        th = jnp.tanh(jnp.dot(lhs, wg_ref[...], preferred_element_type=F32))
        half_xc = 0.5 * xc
        rows = slice(c * REC_CHUNK, (c + 1) * REC_CHUNK)
        for d, (a_ref, u_ref) in enumerate(((a_f, u_f), (a_b, u_b))):
            th_r = th[:, (2 * d) * LANES:(2 * d + 1) * LANES]
            th_i = th[:, (2 * d + 1) * LANES:(2 * d + 2) * LANES]
            scale = half_log2_a[d:d + 1, :]
            a = jnp.exp2(scale * th_r + scale)
            v = 1.0 - a * a
            root = jnp.where(v > 0.0, v * lax.rsqrt(v), 0.0)
            a_ref[rows, :] = a
            u_ref[rows, :] = root * (th_i * half_xc + half_xc)

    def seg_rows(j):
        return pl.ds(j, SCAN_SEGS, stride=SEG_PITCH)

    zeros = jnp.zeros((SCAN_SEGS, LANES), F32)
    ones = jnp.ones((SCAN_SEGS, LANES), F32)

    def segment_maps(j, carry):
        hf, pf, hb, pb = carry
        jb = n_steps - 1 - j
        af = a_f[seg_rows(j), :]
        ab = a_b[seg_rows(jb), :]
        hf = af * hf + u_f[seg_rows(j), :]
        hb = ab * hb + u_b[seg_rows(jb), :]
        return hf, pf * af, hb, pb * ab

    hf, pf, hb, pb = lax.fori_loop(0, n_steps, segment_maps, (zeros, ones, zeros, ones), unroll=4)

    seg = lax.broadcasted_iota(jnp.int32, (SCAN_SEGS, LANES), 0)

    def entering(h, p, toward_higher):
        d = 1
        while d < SCAN_SEGS:
            shift = d if toward_higher else SCAN_SEGS - d
            have = (seg >= d) if toward_higher else (seg < SCAN_SEGS - d)
            h = jnp.where(have, h + p * pltpu.roll(h, shift, 0), h)
            p = jnp.where(have, p * pltpu.roll(p, shift, 0), p)
            d *= 2
        one_step = 1 if toward_higher else SCAN_SEGS - 1
        edge = (seg == 0) if toward_higher else (seg == SCAN_SEGS - 1)
        return jnp.where(edge, 0.0, pltpu.roll(h, one_step, 0))

    hf0 = entering(hf, pf, True)
    hb0 = entering(hb, pb, False)

    def true_states(j, carry):
        hf, hb = carry
        jb = n_steps - 1 - j
        hf = a_f[seg_rows(j), :] * hf + u_f[seg_rows(j), :]
        hb = a_b[seg_rows(jb), :] * hb + u_b[seg_rows(jb), :]
        h_f[seg_rows(j), :] = hf
        h_b[seg_rows(jb), :] = hb
        return hf, hb

    lax.fori_loop(0, n_steps, true_states, (hf0, hb0), unroll=4)

    for c in range(n_chunks):
        rows = slice(c * REC_CHUNK, (c + 1) * REC_CHUNK)
        o_ref[rows, :] = gate_ref[rows, :] * (h_f[rows, :] + h_b[rows, :])


def _rec(l, xr, gate, cw, cb, wg, lam, batch, seq, cast_stacks, cast_gains, cast_layer):
    n_slabs = LRU_WIDTH // LANES
    assert SCAN_SEGS * SEG_PITCH >= seq > (SCAN_SEGS - 1) * SEG_PITCH and seq % REC_CHUNK == 0
    slab = pl.BlockSpec((seq, LANES), lambda b, s: (b, s))
    scan_rows = SCAN_SEGS * SEG_PITCH
    cast_in, cast_out, cast_shapes = _cast_plan(cast_stacks, cast_gains, cast_layer, batch * n_slabs,
                                                lambda b, s: b * n_slabs + s)
    outs = pl.pallas_call(
        functools.partial(_rec_kernel, seq=seq, n_cast=len(cast_stacks)),
        grid=(batch, n_slabs),
        in_specs=[slab, slab,
                  pl.BlockSpec((None, CONV_WIDTH, LANES), lambda b, s: (l, 0, s)),
                  pl.BlockSpec((None, 1, LANES), lambda b, s: (l, 0, s)),
                  pl.BlockSpec((None, None, 2 * LANES, 4 * LANES), lambda b, s: (l, s, 0, 0)),
                  pl.BlockSpec((None, 2, LANES), lambda b, s: (l, 0, s))] + cast_in,
        out_specs=[slab] + cast_out,
        out_shape=[jax.ShapeDtypeStruct((batch * seq, LRU_WIDTH), F32)] + cast_shapes,
        scratch_shapes=[pltpu.VMEM((scan_rows, LANES), F32)] * 6,
        compiler_params=pltpu.CompilerParams(dimension_semantics=("arbitrary", "arbitrary"),
                                             vmem_limit_bytes=VMEM_LIMIT),
        name="rg_lru",
    )(xr, gate, cw, cb, wg, lam, *cast_stacks, *cast_gains)
    return outs[0], outs[1:]


def _bias_kernel(buckets_ref, table_ref, o_ref):
    buckets = buckets_ref[...]
    for h in range(N_HEADS):
        acc = jnp.zeros(buckets.shape, F32)
        for b in range(N_BUCKETS):
            acc = jnp.where(buckets == b, table_ref[b, h] * LOG2E, acc)
        g, p, e = h // KV_GROUP, (h % KV_GROUP) // 2, h % 2
        o_ref[2 * g + e, p * BLOCK:(p + 1) * BLOCK, :] = acc


def _rel_bias_table(buckets, rel_bias):
    return pl.pallas_call(
        _bias_kernel,
        in_specs=[pl.BlockSpec(memory_space=pltpu.VMEM), pl.BlockSpec(memory_space=pltpu.SMEM)],
        out_specs=pl.BlockSpec(memory_space=pltpu.VMEM),
        out_shape=jax.ShapeDtypeStruct((N_HEADS // 2, 2 * BLOCK, 3 * BLOCK), F32),
        name="rel_bias",
    )(buckets, rel_bias)


ATT_SETUP_ROWS = 512
ATT_BLOCKS = 1
ATT_UNROLL = 4
HALF = LANES // 2


def _attn_kernel(q_ref, k_ref, v_ref, bias_ref, sink_ref, o_ref, kvar, vvar, *, seq, layer):
    nb = seq // BLOCK
    lo_half = lax.broadcasted_iota(jnp.int32, (ATT_SETUP_ROWS, LANES), 1) < HALF
    zeros_blk = jnp.zeros((BLOCK, LANES), BF16)
    for u in range(2 * N_KV_HEADS):
        vvar[u, 0:BLOCK, :] = zeros_blk
        vvar[u, BLOCK + seq:2 * BLOCK + seq, :] = zeros_blk
        kvar[u, :, 0:BLOCK] = zeros_blk
        kvar[u, :, BLOCK + seq:2 * BLOCK + seq] = zeros_blk
    no_head = jnp.zeros((HEAD_DIM, ATT_SETUP_ROWS), BF16)
    for r in range(seq // ATT_SETUP_ROWS):
        src = slice(r * ATT_SETUP_ROWS, (r + 1) * ATT_SETUP_ROWS)
        dst = slice(BLOCK + r * ATT_SETUP_ROWS, BLOCK + (r + 1) * ATT_SETUP_ROWS)
        kt = k_ref[src, :].astype(F32).T.astype(BF16)
        vc = v_ref[src, :]
        vr = pltpu.roll(vc, HALF, 1)
        one = jnp.ones_like(vc)
        for g in range(N_KV_HEADS):
            head = kt[g * HEAD_DIM:(g + 1) * HEAD_DIM, :]
            kvar[2 * g, :, dst] = jnp.concatenate([head, no_head], axis=0)
            kvar[2 * g + 1, :, dst] = jnp.concatenate([no_head, head], axis=0)
        vvar[0, dst, :] = jnp.where(lo_half, vc, one)
        vvar[1, dst, :] = jnp.where(lo_half, one, vr)
        vvar[2, dst, :] = jnp.where(lo_half, vr, one)
        vvar[3, dst, :] = jnp.where(lo_half, one, vc)

    assert BLOCK == WINDOW == LANES
    t_idx = lax.broadcasted_iota(jnp.int32, (2 * BLOCK, LANES), 0) % BLOCK
    lane_minus_t = lax.broadcasted_iota(jnp.int32, (2 * BLOCK, LANES), 1) - t_idx
    first_head = lax.broadcasted_iota(jnp.int32, (2 * BLOCK, 1), 0) < BLOCK
    lo_out = lax.broadcasted_iota(jnp.int32, (BLOCK, LANES), 1) < HALF
    units = [(g, e) for g in range(N_KV_HEADS) for e in range(2)]
    never = 2 * LANES

    def blocks(i, carry):
        ns = [i * ATT_BLOCKS + j for j in range(ATT_BLOCKS)]
        starts = [pl.multiple_of(n * BLOCK, BLOCK) for n in ns]
        work = [(j, g, e) for j in range(ATT_BLOCKS) for g, e in units]
        scores = []
        for j, g, e in work:
            qblk = q_ref[pl.ds(starts[j], BLOCK), (2 * g) * LANES:(2 * g + 2) * LANES]
            lhs = jnp.concatenate([qblk[:, :LANES], qblk[:, LANES:]], axis=0)
            kwin = kvar[2 * g + e, :, pl.ds(starts[j], 3 * BLOCK)]
            scores.append(jnp.dot(lhs, kwin, preferred_element_type=F32))
        probs, sink_mass = [], []
        for (j, g, e), s in zip(work, scores):
            u = 2 * g + e
            prev_min = jnp.where(ns[j] == 0, never, 0)
            next_max = jnp.where(ns[j] == nb - 1, -never, 0)
            s_prev = jnp.where(lane_minus_t >= prev_min, s[:, :BLOCK] + bias_ref[u, :, :BLOCK], NEG_INF)
            s_own = s[:, BLOCK:2 * BLOCK] + bias_ref[u, :, BLOCK:2 * BLOCK]
            s_next = jnp.where(lane_minus_t <= next_max, s[:, 2 * BLOCK:] + bias_ref[u, :, 2 * BLOCK:], NEG_INF)
            sink = jnp.where(first_head, sink_ref[layer, 4 * g + e], sink_ref[layer, 4 * g + 2 + e]) * LOG2E
            row_max = jnp.max(jnp.maximum(jnp.maximum(s_prev, s_own), s_next), axis=-1, keepdims=True)
            m = jnp.maximum(row_max, sink)
            p = jnp.concatenate([jnp.exp2(s_prev - m), jnp.exp2(s_own - m), jnp.exp2(s_next - m)], axis=-1)
            probs.append(p.astype(BF16))
            sink_mass.append(jnp.exp2(sink - m))
        outs = []
        for (j, g, e), p in zip(work, probs):
            vwin = vvar[2 * g + e, pl.ds(starts[j], 3 * BLOCK), :]
            outs.append(jnp.dot(p, vwin, preferred_element_type=F32))
        for j in range(ATT_BLOCKS):
            chunks = []
            for g in range(N_KV_HEADS):
                for p in range(2):
                    rows = slice(p * BLOCK, (p + 1) * BLOCK)
                    ue, uo = j * len(units) + 2 * g, j * len(units) + 2 * g + 1
                    even, odd = outs[ue][rows], outs[uo][rows]
                    num = jnp.where(lo_out, even, odd)
                    den = (pltpu.roll(jnp.where(lo_out, odd, even), HALF, 1)
                           + jnp.where(lo_out, sink_mass[ue][rows], sink_mass[uo][rows]))
                    chunks.append(num / den)
            y = jnp.concatenate(chunks, axis=-1)
            o_ref[pl.ds(starts[j], BLOCK), :] = (y * _inv_rms(y)).astype(BF16)
        return carry

    lax.fori_loop(0, nb // ATT_BLOCKS, blocks, 0, unroll=ATT_UNROLL)


def _attn(l, q, k, v, bias, sink, batch, seq):
    assert seq % ATT_SETUP_ROWS == 0

    def seq_rows(width):
        return pl.BlockSpec((seq, width), lambda b: (b, 0))

    return pl.pallas_call(
        functools.partial(_attn_kernel, seq=seq, layer=l),
        grid=(batch,),
        in_specs=[seq_rows(ATT_WIDTH), seq_rows(KV_WIDTH), seq_rows(KV_WIDTH),
                  pl.BlockSpec(bias.shape, lambda b: (0, 0, 0), pipeline_mode=pl.Buffered(1)),
                  pl.BlockSpec(memory_space=pltpu.SMEM)],
        out_specs=seq_rows(ATT_WIDTH),
        out_shape=jax.ShapeDtypeStruct((batch * seq, ATT_WIDTH), BF16),
        scratch_shapes=[pltpu.VMEM((2 * N_KV_HEADS, LANES, seq + 2 * BLOCK), BF16),
                        pltpu.VMEM((2 * N_KV_HEADS, seq + 2 * BLOCK, LANES), BF16)],
        compiler_params=pltpu.CompilerParams(dimension_semantics=("arbitrary",),
                                             vmem_limit_bytes=VMEM_LIMIT),
        name="windowed_gqa",
    )(q, k, v, bias, sink)


def _t5_bucket_index():
    t = jnp.arange(BLOCK)[:, None]
    j = jnp.arange(3 * BLOCK)[None, :]
    rel = j - BLOCK - t
    half = N_BUCKETS // 2
    max_exact = half // 2
    ret = (rel > 0).astype(jnp.int32) * half
    n = jnp.abs(rel)
    n_f = jnp.maximum(n, 1).astype(jnp.float32)
    large = max_exact + (jnp.log(n_f / max_exact) / math.log(MAX_DISTANCE / max_exact)
                         * (half - max_exact)).astype(jnp.int32)
    large = jnp.minimum(large, half - 1)
    return ret + jnp.where(n < max_exact, n, large)


def _gate_weights(w_a, b_a, w_x, b_x):
    depth = w_a.shape[0]
    n_slabs = LRU_WIDTH // LANES
    per_slab = LANES // LRU_BLOCK_W

    def slab_blockdiag(w):
        w = w.reshape(depth, n_slabs, per_slab, LRU_BLOCK_W, LRU_BLOCK_W)
        eye = jnp.eye(per_slab, dtype=w.dtype)
        full = jnp.einsum('lspcd,pq->lspcqd', w, eye)
        return full.reshape(depth, n_slabs, LANES, LANES)

    mats = [slab_blockdiag(w[:, d]) for d in range(2) for w in (w_a, w_x)]
    wg = (0.5 * jnp.concatenate(mats, axis=-1)).astype(BF16)
    biases = [b[:, d].reshape(depth, n_slabs, 1, LANES) for d in range(2) for b in (b_a, b_x)]
    bg = 0.5 * jnp.concatenate(biases, axis=-1)
    bg_hi = bg.astype(BF16)
    bg_lo = (bg - bg_hi.astype(F32)).astype(BF16)
    unused = jnp.zeros((depth, n_slabs, LANES - GATE_BIAS_ROWS, 4 * LANES), BF16)
    return jnp.concatenate([wg, bg_hi, bg_lo, unused], axis=2)


def kernel(x, ffn1_norm, ffn1_w_gate, ffn1_w_up, ffn1_w_down, mix_norm, w_in, conv_w, conv_b, lru_w_a, lru_b_a, lru_w_x, lru_b_x, lru_lambda, attn_sink, rel_bias, lru_out_norm, attn_out_norm, w_out, ffn2_norm, ffn2_w_gate, ffn2_w_up, ffn2_w_down, final_norm):
    batch, seq, d_model = x.shape
    depth = ffn1_norm.shape[0]
    tokens = batch * seq
    assert d_model == D_MODEL and seq % BLOCK == 0 and tokens % FFN1_TILE == 0 and tokens % FFN2_TILE == 0
    xt = x.reshape(tokens, D_MODEL)
    bias = _rel_bias_table(_t5_bucket_index(), rel_bias)
    gate_w = _gate_weights(lru_w_a, lru_b_a, lru_w_x, lru_b_x)
    cb = conv_b.reshape(depth, 1, -1)
    nf = final_norm.reshape(1, -1)
    mix_out_norm = jnp.concatenate([lru_out_norm, attn_out_norm], axis=-1)
    g_gate1, g_ffn1, g_mix, g_gate2, g_ffn2, g_out = jnp.stack(
        [0.5 * ffn1_norm, ffn1_norm, mix_norm, 0.5 * ffn2_norm, ffn2_norm, mix_out_norm])[..., None]
    half_step = jnp.full((depth, D_FF, 1), FFN_RES, F32)
    ffn1_stacks = (ffn1_w_gate, ffn1_w_up, ffn1_w_down, w_in)
    ffn1_gains = (g_gate1, g_ffn1, half_step, g_mix)
    ffn2_stacks = (ffn2_w_gate, ffn2_w_up, ffn2_w_down, w_out)
    ffn2_gains = (g_gate2, g_ffn2, half_step, g_out)
    ffn1_w = tuple((w[0] * g[0]).astype(BF16) for w, g in zip(ffn1_stacks, ffn1_gains))
    for l in range(depth):
        (xt, xr, gate, q, k, v), ffn2_w = _ffn1_inproj(l, xt, *ffn1_w, ffn2_stacks, ffn2_gains)
        last = l == depth - 1
        y_rec, ffn1_w = _rec(l, xr, gate, conv_w, cb, gate_w, lru_lambda, batch, seq,
                             () if last else ffn1_stacks, () if last else ffn1_gains, l + 1)
        y_att = _attn(l, q, k, v, bias, attn_sink, batch, seq)
        xt = _outproj_ffn2(xt, y_rec, y_att, ffn2_w[3], *ffn2_w[:3], nf, final_norm=last)
    return xt.reshape(batch, seq, D_MODEL)
```

```python
import functools
import math

import jax
import jax.numpy as jnp
from jax import lax
from jax.experimental import pallas as pl
from jax.experimental.pallas import tpu as pltpu

F32 = jnp.float32
BF16 = jnp.bfloat16

D_MODEL = 1024
LRU_WIDTH = 512
LRU_BLOCK_W = 64
LRU_C = 8.0
CONV_WIDTH = 4
N_HEADS = 8
N_KV_HEADS = 2
KV_GROUP = N_HEADS // N_KV_HEADS
HEAD_DIM = 64
ATT_WIDTH = N_HEADS * HEAD_DIM
KV_WIDTH = N_KV_HEADS * HEAD_DIM
WINDOW = 128
BLOCK = 128
N_BUCKETS = 32
MAX_DISTANCE = 128
D_FF = 2816
FFN_RES = 0.5
EPS = 1e-6
NEG_INF = -1e30
D_IN = 2 * LRU_WIDTH + ATT_WIDTH + 2 * KV_WIDTH
LOG2E = math.log2(math.e)

LANES = 128
SUBLANES = 8
FF_CHUNK = 256
N_FF_CHUNKS = D_FF // FF_CHUNK
FFN1_TILE = 512
FFN2_TILE = 1024
VMEM_LIMIT = 56 * 1024 * 1024

SCAN_SEGS = 32
SCAN_VREGS = SCAN_SEGS // SUBLANES
SEG_PITCH = 129
REC_CHUNK = 512
GATE_BIAS_ROWS = 2


def _rms(x, g):
    return x * lax.rsqrt(jnp.mean(x * x, axis=-1, keepdims=True) + EPS) * g


def _silu_of_twice(half):
    return half + half * jnp.tanh(half)


def _inv_rms(x):
    return lax.rsqrt(jnp.mean(x * x, axis=-1, keepdims=True) + EPS)


def _swiglu_into(xn_ref, wg_ref, wu_ref, wd_ref, acc_ref, residual, unscaled=None):
    for c in range(N_FF_CHUNKS):
        cols = slice(c * FF_CHUNK, (c + 1) * FF_CHUNK)
        if c == 0 and unscaled is not None:
            raw_ref, inv_rms = unscaled
            half_g = inv_rms * jnp.dot(raw_ref[...], wg_ref[:, cols], preferred_element_type=F32)
            u = inv_rms * jnp.dot(raw_ref[...], wu_ref[:, cols], preferred_element_type=F32)
        else:
            half_g = jnp.dot(xn_ref[...], wg_ref[:, cols], preferred_element_type=F32)
            u = jnp.dot(xn_ref[...], wu_ref[:, cols], preferred_element_type=F32)
        act = (_silu_of_twice(half_g) * u).astype(BF16)
        part = jnp.dot(act, wd_ref[cols, :], preferred_element_type=F32)
        if c == 0:
            acc_ref[...] = residual + part
        else:
            acc_ref[...] += part


N_CAST = 4


def _cast_blocks(src_refs, dst_refs):
    n = len(dst_refs)
    for w, gain, dst in zip(src_refs[:n], src_refs[n:], dst_refs):
        dst[...] = (w[...] * gain[...]).astype(BF16)


def _ffn1_inproj_kernel(x_ref, wg_ref, wu_ref, wd_ref, win_ref, *rest):
    cast_src, rest = rest[:2 * N_CAST], rest[2 * N_CAST:]
    xo_ref, xr_ref, gate_ref, q_ref, k_ref, v_ref = rest[:6]
    cast_dst, (xb_ref, xraw_ref, acc_ref) = rest[6:6 + N_CAST], rest[6 + N_CAST:]
    _cast_blocks(cast_src, cast_dst)
    x = x_ref[...]
    xraw_ref[...] = x.astype(BF16)
    inv = _inv_rms(x)
    xb_ref[...] = (x * inv).astype(BF16)
    _swiglu_into(xb_ref, wg_ref, wu_ref, wd_ref, acc_ref, x, unscaled=(xraw_ref, inv))
    x1 = acc_ref[...]
    xo_ref[...] = x1
    xb_ref[...] = x1.astype(BF16)
    inv1 = _inv_rms(x1)

    def proj(rows, lo, width):
        return inv1[rows] * jnp.dot(xb_ref[rows, :], win_ref[:, lo:lo + width], preferred_element_type=F32)

    every = slice(None)
    tile = 2 * LANES
    for lo in range(0, LRU_WIDTH, tile):
        gate_ref[:, lo:lo + tile] = jax.nn.gelu(proj(every, LRU_WIDTH + lo, tile))
        xr_ref[:, lo:lo + tile] = proj(every, lo, tile)
    o = 2 * LRU_WIDTH
    q_ref[...] = (proj(every, o, ATT_WIDTH) * (HEAD_DIM ** -0.5 * LOG2E)).astype(BF16)
    o += ATT_WIDTH
    half = x_ref.shape[0] // 2
    for rows in (slice(0, half), slice(half, 2 * half)):
        kv = proj(rows, o, 2 * KV_WIDTH)
        k_ref[rows, :] = kv[:, :KV_WIDTH].astype(BF16)
        v_ref[rows, :] = kv[:, KV_WIDTH:].astype(BF16)


def _outproj_ffn2_kernel(x_ref, yr_ref, ya_ref, wo_ref, wg_ref, wu_ref, wd_ref, nf_ref,
                         xo_ref, xb_ref, acc_ref, *, final_norm):
    yr = yr_ref[...]
    y = jnp.concatenate([(yr * _inv_rms(yr)).astype(BF16), ya_ref[...]], axis=1)
    x1 = x_ref[...] + jnp.dot(y, wo_ref[...], preferred_element_type=F32)
    xb_ref[...] = (x1 * _inv_rms(x1)).astype(BF16)
    _swiglu_into(xb_ref, wg_ref, wu_ref, wd_ref, acc_ref, x1)
    x2 = acc_ref[...]
    if final_norm:
        x2 = _rms(x2, nf_ref[...])
    xo_ref[...] = x2


def _whole(arr):
    assert arr.ndim == 2
    return pl.BlockSpec(arr.shape, lambda *_: (0, 0), pipeline_mode=pl.Buffered(1))


def _rows(width, tm):
    return pl.BlockSpec((tm, width), lambda i: (i, 0))


def _cast_plan(stacks, gains, l, n_steps, step_of):
    bf16_rows = 2 * SUBLANES
    w_specs, g_specs, out_specs, out_shapes = [], [], [], []
    for w, gain in zip(stacks, gains):
        _, r, c = w.shape
        assert gain.shape == (w.shape[0], r, 1)
        n_blocks = n_steps
        while r % n_blocks or (r // n_blocks) % bf16_rows:
            n_blocks //= 2
        assert n_blocks >= 1 and n_steps % n_blocks == 0
        rep, rows = n_steps // n_blocks, r // n_blocks
        w_specs.append(pl.BlockSpec((None, rows, c), lambda *g, rep=rep: (l, step_of(*g) // rep, 0)))
        g_specs.append(pl.BlockSpec((None, rows, 1), lambda *g, rep=rep: (l, step_of(*g) // rep, 0)))
        out_specs.append(pl.BlockSpec((rows, c), lambda *g, rep=rep: (step_of(*g) // rep, 0)))
        out_shapes.append(jax.ShapeDtypeStruct((r, c), BF16))
    return w_specs + g_specs, out_specs, out_shapes


def _ffn1_inproj(l, x, wg, wu, wd, win, cast_stacks, cast_gains):
    t = x.shape[0]
    tm = FFN1_TILE
    out_shape = [
        jax.ShapeDtypeStruct((t, D_MODEL), F32),
        jax.ShapeDtypeStruct((t, LRU_WIDTH), F32),
        jax.ShapeDtypeStruct((t, LRU_WIDTH), F32),
        jax.ShapeDtypeStruct((t, ATT_WIDTH), BF16),
        jax.ShapeDtypeStruct((t, KV_WIDTH), BF16),
        jax.ShapeDtypeStruct((t, KV_WIDTH), BF16),
    ]
    out_specs = [_rows(D_MODEL, tm), _rows(LRU_WIDTH, tm), _rows(LRU_WIDTH, tm), _rows(ATT_WIDTH, tm),
                 _rows(KV_WIDTH, tm), _rows(KV_WIDTH, tm)]
    cast_in, cast_out, cast_shapes = _cast_plan(cast_stacks, cast_gains, l, t // tm, lambda i: i)
    outs = pl.pallas_call(
        _ffn1_inproj_kernel,
        grid=(t // tm,),
        in_specs=[_rows(D_MODEL, tm), _whole(wg), _whole(wu), _whole(wd), _whole(win)] + cast_in,
        out_specs=out_specs + cast_out,
        out_shape=out_shape + cast_shapes,
        scratch_shapes=[pltpu.VMEM((tm, D_MODEL), BF16), pltpu.VMEM((tm, D_MODEL), BF16),
                        pltpu.VMEM((tm, D_MODEL), F32)],
        compiler_params=pltpu.CompilerParams(dimension_semantics=("arbitrary",),
                                             vmem_limit_bytes=VMEM_LIMIT),
        name="ffn1_inproj",
    )(x, wg, wu, wd, win, *cast_stacks, *cast_gains)
    return outs[:6], outs[6:]


def _outproj_ffn2(x, yr, ya, wo, wg, wu, wd, nf, final_norm):
    t = x.shape[0]
    tm = FFN2_TILE
    return pl.pallas_call(
        functools.partial(_outproj_ffn2_kernel, final_norm=final_norm),
        grid=(t // tm,),
        in_specs=[_rows(D_MODEL, tm), _rows(LRU_WIDTH, tm), _rows(ATT_WIDTH, tm), _whole(wo),
                  _whole(wg), _whole(wu), _whole(wd), _whole(nf)],
        out_specs=_rows(D_MODEL, tm),
        out_shape=jax.ShapeDtypeStruct((t, D_MODEL), F32),
        scratch_shapes=[pltpu.VMEM((tm, D_MODEL), BF16), pltpu.VMEM((tm, D_MODEL), F32)],
        compiler_params=pltpu.CompilerParams(dimension_semantics=("arbitrary",),
                                             vmem_limit_bytes=VMEM_LIMIT),
        name="outproj_ffn2",
    )(x, yr, ya, wo, wg, wu, wd, nf)


def _rec_kernel(xr_ref, gate_ref, cw_ref, cb_ref, wg_ref, lam_ref, *rest, seq, n_cast):
    cast_src, o_ref = rest[:2 * n_cast], rest[2 * n_cast]
    cast_dst = rest[2 * n_cast + 1:3 * n_cast + 1]
    a_f, u_f, a_b, u_b, h_f, h_b = rest[3 * n_cast + 1:]
    _cast_blocks(cast_src, cast_dst)
    n_chunks = seq // REC_CHUNK
    n_steps = SEG_PITCH
    pad_rows = SCAN_SEGS * SEG_PITCH - seq
    chunk_row = lax.broadcasted_iota(jnp.int32, (REC_CHUNK, LANES), 0)

    def shifted_rows(c, offset):
        lo = c * REC_CHUNK + offset
        if lo >= 0 and lo + REC_CHUNK <= seq:
            return xr_ref[lo:lo + REC_CHUNK, :]
        rolled = pltpu.roll(xr_ref[c * REC_CHUNK:(c + 1) * REC_CHUNK, :], (-offset) % REC_CHUNK, 0)
        outside = (chunk_row < -offset) if offset < 0 else (chunk_row >= REC_CHUNK - offset)
        return jnp.where(outside, 0.0, rolled)

    for a_ref, u_ref in ((a_f, u_f), (a_b, u_b)):
        a_ref[seq:seq + pad_rows, :] = jnp.ones((pad_rows, LANES), F32)
        u_ref[seq:seq + pad_rows, :] = jnp.zeros((pad_rows, LANES), F32)

    lam = lam_ref[...]
    neg_softplus = -(jnp.maximum(-lam, 0.0) + jnp.log1p(jnp.exp(-jnp.abs(lam))))
    half_log2_a = (0.5 * LRU_C * LOG2E) * neg_softplus

    cw = cw_ref[...]
    cb = cb_ref[...]
    bias_cols = (lax.broadcasted_iota(jnp.int32, (REC_CHUNK, LANES), 1) < GATE_BIAS_ROWS).astype(BF16)
    for c in range(n_chunks):
        xc = cb
        for tap in range(CONV_WIDTH):
            xc = xc + cw[tap:tap + 1, :] * shifted_rows(c, tap - CONV_WIDTH // 2)
        lhs = jnp.concatenate([xc.astype(BF16), bias_cols], axis=1)
        th = jnp.tanh(jnp.dot(lhs, wg_ref[...], preferred_element_type=F32))
        half_xc = 0.5 * xc
        rows = slice(c * REC_CHUNK, (c + 1) * REC_CHUNK)
        for d, (a_ref, u_ref) in enumerate(((a_f, u_f), (a_b, u_b))):
            th_r = th[:, (2 * d) * LANES:(2 * d + 1) * LANES]
            th_i = th[:, (2 * d + 1) * LANES:(2 * d + 2) * LANES]
            scale = half_log2_a[d:d + 1, :]
            a = jnp.exp2(scale * th_r + scale)
            v = 1.0 - a * a
            root = jnp.where(v > 0.0, v * lax.rsqrt(v), 0.0)
            a_ref[rows, :] = a
            u_ref[rows, :] = root * (th_i * half_xc + half_xc)

    def seg_rows(j):
        return pl.ds(j, SCAN_SEGS, stride=SEG_PITCH)

    zeros = jnp.zeros((SCAN_SEGS, LANES), F32)
    ones = jnp.ones((SCAN_SEGS, LANES), F32)

    def segment_maps(j, carry):
        hf, pf, hb, pb = carry
        jb = n_steps - 1 - j
        af = a_f[seg_rows(j), :]
        ab = a_b[seg_rows(jb), :]
        hf = af * hf + u_f[seg_rows(j), :]
        hb = ab * hb + u_b[seg_rows(jb), :]
        return hf, pf * af, hb, pb * ab

    hf, pf, hb, pb = lax.fori_loop(0, n_steps, segment_maps, (zeros, ones, zeros, ones), unroll=4)

    seg = lax.broadcasted_iota(jnp.int32, (SCAN_SEGS, LANES), 0)

    def entering(h, p, toward_higher):
        d = 1
        while d < SCAN_SEGS:
            shift = d if toward_higher else SCAN_SEGS - d
            have = (seg >= d) if toward_higher else (seg < SCAN_SEGS - d)
            h = jnp.where(have, h + p * pltpu.roll(h, shift, 0), h)
            p = jnp.where(have, p * pltpu.roll(p, shift, 0), p)
            d *= 2
        one_step = 1 if toward_higher else SCAN_SEGS - 1
        edge = (seg == 0) if toward_higher else (seg == SCAN_SEGS - 1)
        return jnp.where(edge, 0.0, pltpu.roll(h, one_step, 0))

    hf0 = entering(hf, pf, True)
    hb0 = entering(hb, pb, False)

    def true_states(j, carry):
        hf, hb = carry
        jb = n_steps - 1 - j
        hf = a_f[seg_rows(j), :] * hf + u_f[seg_rows(j), :]
        hb = a_b[seg_rows(jb), :] * hb + u_b[seg_rows(jb), :]
        h_f[seg_rows(j), :] = hf
        h_b[seg_rows(jb), :] = hb
        return hf, hb

    lax.fori_loop(0, n_steps, true_states, (hf0, hb0), unroll=4)

    for c in range(n_chunks):
        rows = slice(c * REC_CHUNK, (c + 1) * REC_CHUNK)
        o_ref[rows, :] = gate_ref[rows, :] * (h_f[rows, :] + h_b[rows, :])


def _rec(l, xr, gate, cw, cb, wg, lam, batch, seq, cast_stacks, cast_gains, cast_layer):
    n_slabs = LRU_WIDTH // LANES
    assert SCAN_SEGS * SEG_PITCH >= seq > (SCAN_SEGS - 1) * SEG_PITCH and seq % REC_CHUNK == 0
    slab = pl.BlockSpec((seq, LANES), lambda b, s: (b, s))
    scan_rows = SCAN_SEGS * SEG_PITCH
    cast_in, cast_out, cast_shapes = _cast_plan(cast_stacks, cast_gains, cast_layer, batch * n_slabs,
                                                lambda b, s: b * n_slabs + s)
    outs = pl.pallas_call(
        functools.partial(_rec_kernel, seq=seq, n_cast=len(cast_stacks)),
        grid=(batch, n_slabs),
        in_specs=[slab, slab,
                  pl.BlockSpec((None, CONV_WIDTH, LANES), lambda b, s: (l, 0, s)),
                  pl.BlockSpec((None, 1, LANES), lambda b, s: (l, 0, s)),
                  pl.BlockSpec((None, None, 2 * LANES, 4 * LANES), lambda b, s: (l, s, 0, 0)),
                  pl.BlockSpec((None, 2, LANES), lambda b, s: (l, 0, s))] + cast_in,
        out_specs=[slab] + cast_out,
        out_shape=[jax.ShapeDtypeStruct((batch * seq, LRU_WIDTH), F32)] + cast_shapes,
        scratch_shapes=[pltpu.VMEM((scan_rows, LANES), F32)] * 6,
        compiler_params=pltpu.CompilerParams(dimension_semantics=("arbitrary", "arbitrary"),
                                             vmem_limit_bytes=VMEM_LIMIT),
        name="rg_lru",
    )(xr, gate, cw, cb, wg, lam, *cast_stacks, *cast_gains)
    return outs[0], outs[1:]


def _bias_kernel(buckets_ref, table_ref, o_ref):
    buckets = buckets_ref[...]
    for h in range(N_HEADS):
        acc = jnp.zeros(buckets.shape, F32)
        for b in range(N_BUCKETS):
            acc = jnp.where(buckets == b, table_ref[b, h] * LOG2E, acc)
        g, p, e = h // KV_GROUP, (h % KV_GROUP) // 2, h % 2
        o_ref[2 * g + e, p * BLOCK:(p + 1) * BLOCK, :] = acc


def _rel_bias_table(buckets, rel_bias):
    return pl.pallas_call(
        _bias_kernel,
        in_specs=[pl.BlockSpec(memory_space=pltpu.VMEM), pl.BlockSpec(memory_space=pltpu.SMEM)],
        out_specs=pl.BlockSpec(memory_space=pltpu.VMEM),
        out_shape=jax.ShapeDtypeStruct((N_HEADS // 2, 2 * BLOCK, 3 * BLOCK), F32),
        name="rel_bias",
    )(buckets, rel_bias)


ATT_SETUP_ROWS = 512
ATT_BLOCKS = 1
ATT_UNROLL = 8
HALF = LANES // 2


def _attn_kernel(q_ref, k_ref, v_ref, bias_ref, sink_ref, o_ref, kvar, vvar, *, seq, layer):
    nb = seq // BLOCK
    lo_half = lax.broadcasted_iota(jnp.int32, (ATT_SETUP_ROWS, LANES), 1) < HALF
    zeros_blk = jnp.zeros((BLOCK, LANES), BF16)
    for u in range(2 * N_KV_HEADS):
        vvar[u, 0:BLOCK, :] = zeros_blk
        vvar[u, BLOCK + seq:2 * BLOCK + seq, :] = zeros_blk
        kvar[u, :, 0:BLOCK] = zeros_blk
        kvar[u, :, BLOCK + seq:2 * BLOCK + seq] = zeros_blk
    no_head = jnp.zeros((HEAD_DIM, ATT_SETUP_ROWS), BF16)
    for r in range(seq // ATT_SETUP_ROWS):
        src = slice(r * ATT_SETUP_ROWS, (r + 1) * ATT_SETUP_ROWS)
        dst = slice(BLOCK + r * ATT_SETUP_ROWS, BLOCK + (r + 1) * ATT_SETUP_ROWS)
        kt = k_ref[src, :].astype(F32).T.astype(BF16)
        vc = v_ref[src, :]
        vr = pltpu.roll(vc, HALF, 1)
        one = jnp.ones_like(vc)
        for g in range(N_KV_HEADS):
            head = kt[g * HEAD_DIM:(g + 1) * HEAD_DIM, :]
            kvar[2 * g, :, dst] = jnp.concatenate([head, no_head], axis=0)
            kvar[2 * g + 1, :, dst] = jnp.concatenate([no_head, head], axis=0)
        vvar[0, dst, :] = jnp.where(lo_half, vc, one)
        vvar[1, dst, :] = jnp.where(lo_half, one, vr)
        vvar[2, dst, :] = jnp.where(lo_half, vr, one)
        vvar[3, dst, :] = jnp.where(lo_half, one, vc)

    assert BLOCK == WINDOW == LANES
    t_idx = lax.broadcasted_iota(jnp.int32, (2 * BLOCK, LANES), 0) % BLOCK
    lane_minus_t = lax.broadcasted_iota(jnp.int32, (2 * BLOCK, LANES), 1) - t_idx
    first_head = lax.broadcasted_iota(jnp.int32, (2 * BLOCK, 1), 0) < BLOCK
    lo_out = lax.broadcasted_iota(jnp.int32, (BLOCK, LANES), 1) < HALF
    units = [(g, e) for g in range(N_KV_HEADS) for e in range(2)]
    never = 2 * LANES

    def blocks(i, carry):
        ns = [i * ATT_BLOCKS + j for j in range(ATT_BLOCKS)]
        starts = [pl.multiple_of(n * BLOCK, BLOCK) for n in ns]
        work = [(j, g, e) for j in range(ATT_BLOCKS) for g, e in units]
        scores = []
        for j, g, e in work:
            qblk = q_ref[pl.ds(starts[j], BLOCK), (2 * g) * LANES:(2 * g + 2) * LANES]
            lhs = jnp.concatenate([qblk[:, :LANES], qblk[:, LANES:]], axis=0)
            kwin = kvar[2 * g + e, :, pl.ds(starts[j], 3 * BLOCK)]
            scores.append(jnp.dot(lhs, kwin, preferred_element_type=F32))
        probs, sink_mass = [], []
        for (j, g, e), s in zip(work, scores):
            u = 2 * g + e
            prev_min = jnp.where(ns[j] == 0, never, 0)
            next_max = jnp.where(ns[j] == nb - 1, -never, 0)
            s_prev = jnp.where(lane_minus_t >= prev_min, s[:, :BLOCK] + bias_ref[u, :, :BLOCK], NEG_INF)
            s_own = s[:, BLOCK:2 * BLOCK] + bias_ref[u, :, BLOCK:2 * BLOCK]
            s_next = jnp.where(lane_minus_t <= next_max, s[:, 2 * BLOCK:] + bias_ref[u, :, 2 * BLOCK:], NEG_INF)
            sink = jnp.where(first_head, sink_ref[layer, 4 * g + e], sink_ref[layer, 4 * g + 2 + e]) * LOG2E
            row_max = jnp.max(jnp.maximum(jnp.maximum(s_prev, s_own), s_next), axis=-1, keepdims=True)
            m = jnp.maximum(row_max, sink)
            p = jnp.concatenate([jnp.exp2(s_prev - m), jnp.exp2(s_own - m), jnp.exp2(s_next - m)], axis=-1)
            probs.append(p.astype(BF16))
            sink_mass.append(jnp.exp2(sink - m))
        outs = []
        for (j, g, e), p in zip(work, probs):
            vwin = vvar[2 * g + e, pl.ds(starts[j], 3 * BLOCK), :]
            outs.append(jnp.dot(p, vwin, preferred_element_type=F32))
        for j in range(ATT_BLOCKS):
            chunks = []
            for g in range(N_KV_HEADS):
                for p in range(2):
                    rows = slice(p * BLOCK, (p + 1) * BLOCK)
                    ue, uo = j * len(units) + 2 * g, j * len(units) + 2 * g + 1
                    even, odd = outs[ue][rows], outs[uo][rows]
                    num = jnp.where(lo_out, even, odd)
                    den = (pltpu.roll(jnp.where(lo_out, odd, even), HALF, 1)
                           + jnp.where(lo_out, sink_mass[ue][rows], sink_mass[uo][rows]))
                    chunks.append(num / den)
            y = jnp.concatenate(chunks, axis=-1)
            o_ref[pl.ds(starts[j], BLOCK), :] = (y * _inv_rms(y)).astype(BF16)
        return carry

    lax.fori_loop(0, nb // ATT_BLOCKS, blocks, 0, unroll=ATT_UNROLL)


def _attn(l, q, k, v, bias, sink, batch, seq):
    assert seq % ATT_SETUP_ROWS == 0

    def seq_rows(width):
        return pl.BlockSpec((seq, width), lambda b: (b, 0))

    return pl.pallas_call(
        functools.partial(_attn_kernel, seq=seq, layer=l),
        grid=(batch,),
        in_specs=[seq_rows(ATT_WIDTH), seq_rows(KV_WIDTH), seq_rows(KV_WIDTH),
                  pl.BlockSpec(bias.shape, lambda b: (0, 0, 0), pipeline_mode=pl.Buffered(1)),
                  pl.BlockSpec(memory_space=pltpu.SMEM)],
        out_specs=seq_rows(ATT_WIDTH),
        out_shape=jax.ShapeDtypeStruct((batch * seq, ATT_WIDTH), BF16),
        scratch_shapes=[pltpu.VMEM((2 * N_KV_HEADS, LANES, seq + 2 * BLOCK), BF16),
                        pltpu.VMEM((2 * N_KV_HEADS, seq + 2 * BLOCK, LANES), BF16)],
        compiler_params=pltpu.CompilerParams(dimension_semantics=("arbitrary",),
                                             vmem_limit_bytes=VMEM_LIMIT),
        name="windowed_gqa",
    )(q, k, v, bias, sink)


def _t5_bucket_index():
    t = jnp.arange(BLOCK)[:, None]
    j = jnp.arange(3 * BLOCK)[None, :]
    rel = j - BLOCK - t
    half = N_BUCKETS // 2
    max_exact = half // 2
    ret = (rel > 0).astype(jnp.int32) * half
    n = jnp.abs(rel)
    n_f = jnp.maximum(n, 1).astype(jnp.float32)
    large = max_exact + (jnp.log(n_f / max_exact) / math.log(MAX_DISTANCE / max_exact)
                         * (half - max_exact)).astype(jnp.int32)
    large = jnp.minimum(large, half - 1)
    return ret + jnp.where(n < max_exact, n, large)


def _gate_weights(w_a, b_a, w_x, b_x):
    depth = w_a.shape[0]
    n_slabs = LRU_WIDTH // LANES
    per_slab = LANES // LRU_BLOCK_W

    def slab_blockdiag(w):
        w = w.reshape(depth, n_slabs, per_slab, LRU_BLOCK_W, LRU_BLOCK_W)
        eye = jnp.eye(per_slab, dtype=w.dtype)
        full = jnp.einsum('lspcd,pq->lspcqd', w, eye)
        return full.reshape(depth, n_slabs, LANES, LANES)

    mats = [slab_blockdiag(w[:, d]) for d in range(2) for w in (w_a, w_x)]
    wg = (0.5 * jnp.concatenate(mats, axis=-1)).astype(BF16)
    biases = [b[:, d].reshape(depth, n_slabs, 1, LANES) for d in range(2) for b in (b_a, b_x)]
    bg = 0.5 * jnp.concatenate(biases, axis=-1)
    bg_hi = bg.astype(BF16)
    bg_lo = (bg - bg_hi.astype(F32)).astype(BF16)
    unused = jnp.zeros((depth, n_slabs, LANES - GATE_BIAS_ROWS, 4 * LANES), BF16)
    return jnp.concatenate([wg, bg_hi, bg_lo, unused], axis=2)


def kernel(x, ffn1_norm, ffn1_w_gate, ffn1_w_up, ffn1_w_down, mix_norm, w_in, conv_w, conv_b, lru_w_a, lru_b_a, lru_w_x, lru_b_x, lru_lambda, attn_sink, rel_bias, lru_out_norm, attn_out_norm, w_out, ffn2_norm, ffn2_w_gate, ffn2_w_up, ffn2_w_down, final_norm):
    batch, seq, d_model = x.shape
    depth = ffn1_norm.shape[0]
    tokens = batch * seq
    assert d_model == D_MODEL and seq % BLOCK == 0 and tokens % FFN1_TILE == 0 and tokens % FFN2_TILE == 0
    xt = x.reshape(tokens, D_MODEL)
    bias = _rel_bias_table(_t5_bucket_index(), rel_bias)
    gate_w = _gate_weights(lru_w_a, lru_b_a, lru_w_x, lru_b_x)
    cb = conv_b.reshape(depth, 1, -1)
    nf = final_norm.reshape(1, -1)
    mix_out_norm = jnp.concatenate([lru_out_norm, attn_out_norm], axis=-1)
    g_gate1, g_ffn1, g_mix, g_gate2, g_ffn2, g_out = jnp.stack(
        [0.5 * ffn1_norm, ffn1_norm, mix_norm, 0.5 * ffn2_norm, ffn2_norm, mix_out_norm])[..., None]
    half_step = jnp.full((depth, D_FF, 1), FFN_RES, F32)
    ffn1_stacks = (ffn1_w_gate, ffn1_w_up, ffn1_w_down, w_in)
    ffn1_gains = (g_gate1, g_ffn1, half_step, g_mix)
    ffn2_stacks = (ffn2_w_gate, ffn2_w_up, ffn2_w_down, w_out)
    ffn2_gains = (g_gate2, g_ffn2, half_step, g_out)
    ffn1_w = tuple((w[0] * g[0]).astype(BF16) for w, g in zip(ffn1_stacks, ffn1_gains))
    for l in range(depth):
        (xt, xr, gate, q, k, v), ffn2_w = _ffn1_inproj(l, xt, *ffn1_w, ffn2_stacks, ffn2_gains)
        last = l == depth - 1
        y_rec, ffn1_w = _rec(l, xr, gate, conv_w, cb, gate_w, lru_lambda, batch, seq,
                             () if last else ffn1_stacks, () if last else ffn1_gains, l + 1)
        y_att = _attn(l, q, k, v, bias, attn_sink, batch, seq)
        xt = _outproj_ffn2(xt, y_rec, y_att, ffn2_w[3], *ffn2_w[:3], nf, final_norm=last)
    return xt.reshape(batch, seq, D_MODEL)
```

```python
import functools
import math

import jax
import jax.numpy as jnp
from jax import lax
from jax.experimental import pallas as pl
from jax.experimental.pallas import tpu as pltpu

F32 = jnp.float32
BF16 = jnp.bfloat16

D_MODEL = 1024
LRU_WIDTH = 512
LRU_BLOCK_W = 64
LRU_C = 8.0
CONV_WIDTH = 4
N_HEADS = 8
N_KV_HEADS = 2
KV_GROUP = N_HEADS // N_KV_HEADS
HEAD_DIM = 64
ATT_WIDTH = N_HEADS * HEAD_DIM
KV_WIDTH = N_KV_HEADS * HEAD_DIM
WINDOW = 128
BLOCK = 128
N_BUCKETS = 32
MAX_DISTANCE = 128
D_FF = 2816
FFN_RES = 0.5
EPS = 1e-6
NEG_INF = -1e30
D_IN = 2 * LRU_WIDTH + ATT_WIDTH + 2 * KV_WIDTH
LOG2E = math.log2(math.e)

LANES = 128
SUBLANES = 8
FF_CHUNK = 256
N_FF_CHUNKS = D_FF // FF_CHUNK
FFN1_TILE = 512
FFN2_TILE = 1024
VMEM_LIMIT = 56 * 1024 * 1024

SCAN_SEGS = 32
SCAN_VREGS = SCAN_SEGS // SUBLANES
SEG_PITCH = 129
REC_CHUNK = 512
GATE_BIAS_ROWS = 2


def _rms(x, g):
    return x * lax.rsqrt(jnp.mean(x * x, axis=-1, keepdims=True) + EPS) * g


def _silu_of_twice(half):
    return half + half * jnp.tanh(half)


def _inv_rms(x):
    return lax.rsqrt(jnp.mean(x * x, axis=-1, keepdims=True) + EPS)


def _swiglu_into(xn_ref, wg_ref, wu_ref, wd_ref, acc_ref, residual, unscaled=None):
    for c in range(N_FF_CHUNKS):
        cols = slice(c * FF_CHUNK, (c + 1) * FF_CHUNK)
        if c == 0 and unscaled is not None:
            raw_ref, inv_rms = unscaled
            half_g = inv_rms * jnp.dot(raw_ref[...], wg_ref[:, cols], preferred_element_type=F32)
            u = inv_rms * jnp.dot(raw_ref[...], wu_ref[:, cols], preferred_element_type=F32)
        else:
            half_g = jnp.dot(xn_ref[...], wg_ref[:, cols], preferred_element_type=F32)
            u = jnp.dot(xn_ref[...], wu_ref[:, cols], preferred_element_type=F32)
        act = (_silu_of_twice(half_g) * u).astype(BF16)
        part = jnp.dot(act, wd_ref[cols, :], preferred_element_type=F32)
        if c == 0:
            acc_ref[...] = residual + part
        else:
            acc_ref[...] += part


N_CAST = 4


def _cast_blocks(src_refs, dst_refs):
    n = len(dst_refs)
    for w, gain, dst in zip(src_refs[:n], src_refs[n:], dst_refs):
        dst[...] = (w[...] * gain[...]).astype(BF16)


def _ffn1_inproj_kernel(x_ref, wg_ref, wu_ref, wd_ref, win_ref, *rest):
    cast_src, rest = rest[:2 * N_CAST], rest[2 * N_CAST:]
    xo_ref, xr_ref, gate_ref, q_ref, k_ref, v_ref = rest[:6]
    cast_dst, (xb_ref, xraw_ref, acc_ref) = rest[6:6 + N_CAST], rest[6 + N_CAST:]
    _cast_blocks(cast_src, cast_dst)
    x = x_ref[...]
    xraw_ref[...] = x.astype(BF16)
    inv = _inv_rms(x)
    xb_ref[...] = (x * inv).astype(BF16)
    _swiglu_into(xb_ref, wg_ref, wu_ref, wd_ref, acc_ref, x, unscaled=(xraw_ref, inv))
    x1 = acc_ref[...]
    xo_ref[...] = x1
    xb_ref[...] = x1.astype(BF16)
    inv1 = _inv_rms(x1)

    def proj(rows, lo, width):
        return inv1[rows] * jnp.dot(xb_ref[rows, :], win_ref[:, lo:lo + width], preferred_element_type=F32)

    every = slice(None)
    tile = 2 * LANES
    for lo in range(0, LRU_WIDTH, tile):
        gate_ref[:, lo:lo + tile] = jax.nn.gelu(proj(every, LRU_WIDTH + lo, tile))
        xr_ref[:, lo:lo + tile] = proj(every, lo, tile)
    o = 2 * LRU_WIDTH
    q_ref[...] = (proj(every, o, ATT_WIDTH) * (HEAD_DIM ** -0.5 * LOG2E)).astype(BF16)
    o += ATT_WIDTH
    half = x_ref.shape[0] // 2
    for rows in (slice(0, half), slice(half, 2 * half)):
        kv = proj(rows, o, 2 * KV_WIDTH)
        k_ref[rows, :] = kv[:, :KV_WIDTH].astype(BF16)
        v_ref[rows, :] = kv[:, KV_WIDTH:].astype(BF16)


def _outproj_ffn2_kernel(x_ref, yr_ref, ya_ref, wo_ref, wg_ref, wu_ref, wd_ref, nf_ref,
                         xo_ref, xb_ref, acc_ref, *, final_norm):
    yr = yr_ref[...]
    y = jnp.concatenate([(yr * _inv_rms(yr)).astype(BF16), ya_ref[...]], axis=1)
    x1 = x_ref[...] + jnp.dot(y, wo_ref[...], preferred_element_type=F32)
    xb_ref[...] = (x1 * _inv_rms(x1)).astype(BF16)
    _swiglu_into(xb_ref, wg_ref, wu_ref, wd_ref, acc_ref, x1)
    x2 = acc_ref[...]
    if final_norm:
        x2 = _rms(x2, nf_ref[...])
    xo_ref[...] = x2


def _whole(arr):
    assert arr.ndim == 2
    return pl.BlockSpec(arr.shape, lambda *_: (0, 0), pipeline_mode=pl.Buffered(1))


def _rows(width, tm):
    return pl.BlockSpec((tm, width), lambda i: (i, 0))


def _cast_plan(stacks, gains, l, n_steps, step_of):
    bf16_rows = 2 * SUBLANES
    w_specs, g_specs, out_specs, out_shapes = [], [], [], []
    for w, gain in zip(stacks, gains):
        _, r, c = w.shape
        assert gain.shape == (w.shape[0], r, 1)
        n_blocks = n_steps
        while r % n_blocks or (r // n_blocks) % bf16_rows:
            n_blocks //= 2
        assert n_blocks >= 1 and n_steps % n_blocks == 0
        rep, rows = n_steps // n_blocks, r // n_blocks
        w_specs.append(pl.BlockSpec((None, rows, c), lambda *g, rep=rep: (l, step_of(*g) // rep, 0)))
        g_specs.append(pl.BlockSpec((None, rows, 1), lambda *g, rep=rep: (l, step_of(*g) // rep, 0)))
        out_specs.append(pl.BlockSpec((rows, c), lambda *g, rep=rep: (step_of(*g) // rep, 0)))
        out_shapes.append(jax.ShapeDtypeStruct((r, c), BF16))
    return w_specs + g_specs, out_specs, out_shapes


def _ffn1_inproj(l, x, wg, wu, wd, win, cast_stacks, cast_gains):
    t = x.shape[0]
    tm = FFN1_TILE
    out_shape = [
        jax.ShapeDtypeStruct((t, D_MODEL), F32),
        jax.ShapeDtypeStruct((t, LRU_WIDTH), F32),
        jax.ShapeDtypeStruct((t, LRU_WIDTH), F32),
        jax.ShapeDtypeStruct((t, ATT_WIDTH), BF16),
        jax.ShapeDtypeStruct((t, KV_WIDTH), BF16),
        jax.ShapeDtypeStruct((t, KV_WIDTH), BF16),
    ]
    out_specs = [_rows(D_MODEL, tm), _rows(LRU_WIDTH, tm), _rows(LRU_WIDTH, tm), _rows(ATT_WIDTH, tm),
                 _rows(KV_WIDTH, tm), _rows(KV_WIDTH, tm)]
    cast_in, cast_out, cast_shapes = _cast_plan(cast_stacks, cast_gains, l, t // tm, lambda i: i)
    outs = pl.pallas_call(
        _ffn1_inproj_kernel,
        grid=(t // tm,),
        in_specs=[_rows(D_MODEL, tm), _whole(wg), _whole(wu), _whole(wd), _whole(win)] + cast_in,
        out_specs=out_specs + cast_out,
        out_shape=out_shape + cast_shapes,
        scratch_shapes=[pltpu.VMEM((tm, D_MODEL), BF16), pltpu.VMEM((tm, D_MODEL), BF16),
                        pltpu.VMEM((tm, D_MODEL), F32)],
        compiler_params=pltpu.CompilerParams(dimension_semantics=("arbitrary",),
                                             vmem_limit_bytes=VMEM_LIMIT),
        name="ffn1_inproj",
    )(x, wg, wu, wd, win, *cast_stacks, *cast_gains)
    return outs[:6], outs[6:]


def _outproj_ffn2(x, yr, ya, wo, wg, wu, wd, nf, final_norm):
    t = x.shape[0]
    tm = FFN2_TILE
    return pl.pallas_call(
        functools.partial(_outproj_ffn2_kernel, final_norm=final_norm),
        grid=(t // tm,),
        in_specs=[_rows(D_MODEL, tm), _rows(LRU_WIDTH, tm), _rows(ATT_WIDTH, tm), _whole(wo),
                  _whole(wg), _whole(wu), _whole(wd), _whole(nf)],
        out_specs=_rows(D_MODEL, tm),
        out_shape=jax.ShapeDtypeStruct((t, D_MODEL), F32),
        scratch_shapes=[pltpu.VMEM((tm, D_MODEL), BF16), pltpu.VMEM((tm, D_MODEL), F32)],
        compiler_params=pltpu.CompilerParams(dimension_semantics=("arbitrary",),
                                             vmem_limit_bytes=VMEM_LIMIT),
        name="outproj_ffn2",
    )(x, yr, ya, wo, wg, wu, wd, nf)


def _rec_kernel(xr_ref, gate_ref, cw_ref, cb_ref, wg_ref, lam_ref, *rest, seq, n_cast):
    cast_src, o_ref = rest[:2 * n_cast], rest[2 * n_cast]
    cast_dst = rest[2 * n_cast + 1:3 * n_cast + 1]
    a_f, u_f, a_b, u_b, h_f, h_b = rest[3 * n_cast + 1:]
    _cast_blocks(cast_src, cast_dst)
    n_chunks = seq // REC_CHUNK
    n_steps = SEG_PITCH
    pad_rows = SCAN_SEGS * SEG_PITCH - seq
    chunk_row = lax.broadcasted_iota(jnp.int32, (REC_CHUNK, LANES), 0)

    def shifted_rows(c, offset):
        lo = c * REC_CHUNK + offset
        if lo >= 0 and lo + REC_CHUNK <= seq:
            return xr_ref[lo:lo + REC_CHUNK, :]
        rolled = pltpu.roll(xr_ref[c * REC_CHUNK:(c + 1) * REC_CHUNK, :], (-offset) % REC_CHUNK, 0)
        outside = (chunk_row < -offset) if offset < 0 else (chunk_row >= REC_CHUNK - offset)
        return jnp.where(outside, 0.0, rolled)

    for a_ref, u_ref in ((a_f, u_f), (a_b, u_b)):
        a_ref[seq:seq + pad_rows, :] = jnp.ones((pad_rows, LANES), F32)
        u_ref[seq:seq + pad_rows, :] = jnp.zeros((pad_rows, LANES), F32)

    lam = lam_ref[...]
    neg_softplus = -(jnp.maximum(-lam, 0.0) + jnp.log1p(jnp.exp(-jnp.abs(lam))))
    half_log2_a = (0.5 * LRU_C * LOG2E) * neg_softplus

    cw = cw_ref[...]
    cb = cb_ref[...]
    bias_cols = (lax.broadcasted_iota(jnp.int32, (REC_CHUNK, LANES), 1) < GATE_BIAS_ROWS).astype(BF16)
    for c in range(n_chunks):
        xc = cb
        for tap in range(CONV_WIDTH):
            xc = xc + cw[tap:tap + 1, :] * shifted_rows(c, tap - CONV_WIDTH // 2)
        lhs = jnp.concatenate([xc.astype(BF16), bias_cols], axis=1)
        th = jnp.tanh(jnp.dot(lhs, wg_ref[...], preferred_element_type=F32))
        half_xc = 0.5 * xc
        rows = slice(c * REC_CHUNK, (c + 1) * REC_CHUNK)
        for d, (a_ref, u_ref) in enumerate(((a_f, u_f), (a_b, u_b))):
            th_r = th[:, (2 * d) * LANES:(2 * d + 1) * LANES]
            th_i = th[:, (2 * d + 1) * LANES:(2 * d + 2) * LANES]
            scale = half_log2_a[d:d + 1, :]
            a = jnp.exp2(scale * th_r + scale)
            v = 1.0 - a * a
            root = jnp.where(v > 0.0, v * lax.rsqrt(v), 0.0)
            a_ref[rows, :] = a
            u_ref[rows, :] = root * (th_i * half_xc + half_xc)

    def seg_rows(j):
        return pl.ds(j, SCAN_SEGS, stride=SEG_PITCH)

    zeros = jnp.zeros((SCAN_SEGS, LANES), F32)
    ones = jnp.ones((SCAN_SEGS, LANES), F32)

    def segment_maps(j, carry):
        hf, pf, hb, pb = carry
        jb = n_steps - 1 - j
        af = a_f[seg_rows(j), :]
        ab = a_b[seg_rows(jb), :]
        hf = af * hf + u_f[seg_rows(j), :]
        hb = ab * hb + u_b[seg_rows(jb), :]
        return hf, pf * af, hb, pb * ab

    hf, pf, hb, pb = lax.fori_loop(0, n_steps, segment_maps, (zeros, ones, zeros, ones), unroll=4)

    seg = lax.broadcasted_iota(jnp.int32, (SCAN_SEGS, LANES), 0)

    def entering(h, p, toward_higher):
        d = 1
        while d < SCAN_SEGS:
            shift = d if toward_higher else SCAN_SEGS - d
            have = (seg >= d) if toward_higher else (seg < SCAN_SEGS - d)
            h = jnp.where(have, h + p * pltpu.roll(h, shift, 0), h)
            p = jnp.where(have, p * pltpu.roll(p, shift, 0), p)
            d *= 2
        one_step = 1 if toward_higher else SCAN_SEGS - 1
        edge = (seg == 0) if toward_higher else (seg == SCAN_SEGS - 1)
        return jnp.where(edge, 0.0, pltpu.roll(h, one_step, 0))

    hf0 = entering(hf, pf, True)
    hb0 = entering(hb, pb, False)

    def true_states(j, carry):
        hf, hb = carry
        jb = n_steps - 1 - j
        hf = a_f[seg_rows(j), :] * hf + u_f[seg_rows(j), :]
        hb = a_b[seg_rows(jb), :] * hb + u_b[seg_rows(jb), :]
        h_f[seg_rows(j), :] = hf
        h_b[seg_rows(jb), :] = hb
        return hf, hb

    lax.fori_loop(0, n_steps, true_states, (hf0, hb0), unroll=4)

    for c in range(n_chunks):
        rows = slice(c * REC_CHUNK, (c + 1) * REC_CHUNK)
        o_ref[rows, :] = gate_ref[rows, :] * (h_f[rows, :] + h_b[rows, :])


def _rec(l, xr, gate, cw, cb, wg, lam, batch, seq, cast_stacks, cast_gains, cast_layer):
    n_slabs = LRU_WIDTH // LANES
    assert SCAN_SEGS * SEG_PITCH >= seq > (SCAN_SEGS - 1) * SEG_PITCH and seq % REC_CHUNK == 0
    slab = pl.BlockSpec((seq, LANES), lambda b, s: (b, s))
    scan_rows = SCAN_SEGS * SEG_PITCH
    cast_in, cast_out, cast_shapes = _cast_plan(cast_stacks, cast_gains, cast_layer, batch * n_slabs,
                                                lambda b, s: b * n_slabs + s)
    outs = pl.pallas_call(
        functools.partial(_rec_kernel, seq=seq, n_cast=len(cast_stacks)),
        grid=(batch, n_slabs),
        in_specs=[slab, slab,
                  pl.BlockSpec((None, CONV_WIDTH, LANES), lambda b, s: (l, 0, s)),
                  pl.BlockSpec((None, 1, LANES), lambda b, s: (l, 0, s)),
                  pl.BlockSpec((None, None, 2 * LANES, 4 * LANES), lambda b, s: (l, s, 0, 0)),
                  pl.BlockSpec((None, 2, LANES), lambda b, s: (l, 0, s))] + cast_in,
        out_specs=[slab] + cast_out,
        out_shape=[jax.ShapeDtypeStruct((batch * seq, LRU_WIDTH), F32)] + cast_shapes,
        scratch_shapes=[pltpu.VMEM((scan_rows, LANES), F32)] * 6,
        compiler_params=pltpu.CompilerParams(dimension_semantics=("arbitrary", "arbitrary"),
                                             vmem_limit_bytes=VMEM_LIMIT),
        name="rg_lru",
    )(xr, gate, cw, cb, wg, lam, *cast_stacks, *cast_gains)
    return outs[0], outs[1:]


def _bias_kernel(buckets_ref, table_ref, o_ref):
    buckets = buckets_ref[...]
    for h in range(N_HEADS):
        acc = jnp.zeros(buckets.shape, F32)
        for b in range(N_BUCKETS):
            acc = jnp.where(buckets == b, table_ref[b, h] * LOG2E, acc)
        g, p, e = h // KV_GROUP, (h % KV_GROUP) // 2, h % 2
        o_ref[2 * g + e, p * BLOCK:(p + 1) * BLOCK, :] = acc


def _rel_bias_table(buckets, rel_bias):
    return pl.pallas_call(
        _bias_kernel,
        in_specs=[pl.BlockSpec(memory_space=pltpu.VMEM), pl.BlockSpec(memory_space=pltpu.SMEM)],
        out_specs=pl.BlockSpec(memory_space=pltpu.VMEM),
        out_shape=jax.ShapeDtypeStruct((N_HEADS // 2, 2 * BLOCK, 3 * BLOCK), F32),
        name="rel_bias",
    )(buckets, rel_bias)


ATT_SETUP_ROWS = 512
ATT_BLOCKS = 1
ATT_UNROLL = 4
HALF = LANES // 2


def _attn_kernel(q_ref, k_ref, v_ref, bias_ref, sink_ref, o_ref, kvar, vvar, *, seq, layer):
    nb = seq // BLOCK
    lo_half = lax.broadcasted_iota(jnp.int32, (ATT_SETUP_ROWS, LANES), 1) < HALF
    zeros_blk = jnp.zeros((BLOCK, LANES), BF16)
    for u in range(2 * N_KV_HEADS):
        vvar[u, 0:BLOCK, :] = zeros_blk
        vvar[u, BLOCK + seq:2 * BLOCK + seq, :] = zeros_blk
        kvar[u, :, 0:BLOCK] = zeros_blk
        kvar[u, :, BLOCK + seq:2 * BLOCK + seq] = zeros_blk
    no_head = jnp.zeros((HEAD_DIM, ATT_SETUP_ROWS), BF16)
    for r in range(seq // ATT_SETUP_ROWS):
        src = slice(r * ATT_SETUP_ROWS, (r + 1) * ATT_SETUP_ROWS)
        dst = slice(BLOCK + r * ATT_SETUP_ROWS, BLOCK + (r + 1) * ATT_SETUP_ROWS)
        kt = k_ref[src, :].astype(F32).T.astype(BF16)
        vc = v_ref[src, :]
        vr = pltpu.roll(vc, HALF, 1)
        one = jnp.ones_like(vc)
        for g in range(N_KV_HEADS):
            head = kt[g * HEAD_DIM:(g + 1) * HEAD_DIM, :]
            kvar[2 * g, :, dst] = jnp.concatenate([head, no_head], axis=0)
            kvar[2 * g + 1, :, dst] = jnp.concatenate([no_head, head], axis=0)
        vvar[0, dst, :] = jnp.where(lo_half, vc, one)
        vvar[1, dst, :] = jnp.where(lo_half, one, vr)
        vvar[2, dst, :] = jnp.where(lo_half, vr, one)
        vvar[3, dst, :] = jnp.where(lo_half, one, vc)

    assert BLOCK == WINDOW == LANES
    t_idx = lax.broadcasted_iota(jnp.int32, (2 * BLOCK, LANES), 0) % BLOCK
    lane_minus_t = lax.broadcasted_iota(jnp.int32, (2 * BLOCK, LANES), 1) - t_idx
    first_head = lax.broadcasted_iota(jnp.int32, (2 * BLOCK, 1), 0) < BLOCK
    lo_out = lax.broadcasted_iota(jnp.int32, (BLOCK, LANES), 1) < HALF
    units = [(g, e) for g in range(N_KV_HEADS) for e in range(2)]
    never = 2 * LANES

    def blocks(i, carry):
        ns = [i * ATT_BLOCKS + j for j in range(ATT_BLOCKS)]
        starts = [pl.multiple_of(n * BLOCK, BLOCK) for n in ns]
        work = [(j, g, e) for j in range(ATT_BLOCKS) for g, e in units]
        scores = []
        for j, g, e in work:
            qblk = q_ref[pl.ds(starts[j], BLOCK), (2 * g) * LANES:(2 * g + 2) * LANES]
            lhs = jnp.concatenate([qblk[:, :LANES], qblk[:, LANES:]], axis=0)
            kwin = kvar[2 * g + e, :, pl.ds(starts[j], 3 * BLOCK)]
            scores.append(jnp.dot(lhs, kwin, preferred_element_type=F32))
        probs, sink_mass = [], []
        for (j, g, e), s in zip(work, scores):
            u = 2 * g + e
            prev_min = jnp.where(ns[j] == 0, never, 0)
            next_max = jnp.where(ns[j] == nb - 1, -never, 0)
            s_prev = jnp.where(lane_minus_t >= prev_min, s[:, :BLOCK] + bias_ref[u, :, :BLOCK], NEG_INF)
            s_own = s[:, BLOCK:2 * BLOCK] + bias_ref[u, :, BLOCK:2 * BLOCK]
            s_next = jnp.where(lane_minus_t <= next_max, s[:, 2 * BLOCK:] + bias_ref[u, :, 2 * BLOCK:], NEG_INF)
            sink = jnp.where(first_head, sink_ref[layer, 4 * g + e], sink_ref[layer, 4 * g + 2 + e]) * LOG2E
            row_max = jnp.max(jnp.maximum(jnp.maximum(s_prev, s_own), s_next), axis=-1, keepdims=True)
            m = jnp.maximum(row_max, sink)
            p = jnp.concatenate([jnp.exp2(s_prev - m), jnp.exp2(s_own - m), jnp.exp2(s_next - m)], axis=-1)
            probs.append(p.astype(BF16))
            sink_mass.append(jnp.exp2(sink - m))
        outs = []
        for (j, g, e), p in zip(work, probs):
            vwin = vvar[2 * g + e, pl.ds(starts[j], 3 * BLOCK), :]
            outs.append(jnp.dot(p, vwin, preferred_element_type=F32))
        for j in range(ATT_BLOCKS):
            chunks = []
            for g in range(N_KV_HEADS):
                for p in range(2):
                    rows = slice(p * BLOCK, (p + 1) * BLOCK)
                    ue, uo = j * len(units) + 2 * g, j * len(units) + 2 * g + 1
                    even, odd = outs[ue][rows], outs[uo][rows]
                    num = jnp.where(lo_out, even, odd)
                    den = (pltpu.roll(jnp.where(lo_out, odd, even), HALF, 1)
                           + jnp.where(lo_out, sink_mass[ue][rows], sink_mass[uo][rows]))
                    chunks.append(num / den)
            y = jnp.concatenate(chunks, axis=-1)
            o_ref[pl.ds(starts[j], BLOCK), :] = (y * _inv_rms(y)).astype(BF16)
        return carry

    lax.fori_loop(0, nb // ATT_BLOCKS, blocks, 0, unroll=ATT_UNROLL)


def _attn(l, q, k, v, bias, sink, batch, seq):
    assert seq % ATT_SETUP_ROWS == 0

    def seq_rows(width):
        return pl.BlockSpec((seq, width), lambda b: (b, 0))

    return pl.pallas_call(
        functools.partial(_attn_kernel, seq=seq, layer=l),
        grid=(batch,),
        in_specs=[seq_rows(ATT_WIDTH), seq_rows(KV_WIDTH), seq_rows(KV_WIDTH),
                  pl.BlockSpec(bias.shape, lambda b: (0, 0, 0), pipeline_mode=pl.Buffered(1)),
                  pl.BlockSpec(memory_space=pltpu.SMEM)],
        out_specs=seq_rows(ATT_WIDTH),
        out_shape=jax.ShapeDtypeStruct((batch * seq, ATT_WIDTH), BF16),
        scratch_shapes=[pltpu.VMEM((2 * N_KV_HEADS, LANES, seq + 2 * BLOCK), BF16),
                        pltpu.VMEM((2 * N_KV_HEADS, seq + 2 * BLOCK, LANES), BF16)],
        compiler_params=pltpu.CompilerParams(dimension_semantics=("arbitrary",),
                                             vmem_limit_bytes=VMEM_LIMIT),
        name="windowed_gqa",
    )(q, k, v, bias, sink)


def _t5_bucket_index():
    t = jnp.arange(BLOCK)[:, None]
    j = jnp.arange(3 * BLOCK)[None, :]
    rel = j - BLOCK - t
    half = N_BUCKETS // 2
    max_exact = half // 2
    ret = (rel > 0).astype(jnp.int32) * half
    n = jnp.abs(rel)
    n_f = jnp.maximum(n, 1).astype(jnp.float32)
    large = max_exact + (jnp.log(n_f / max_exact) / math.log(MAX_DISTANCE / max_exact)
                         * (half - max_exact)).astype(jnp.int32)
    large = jnp.minimum(large, half - 1)
    return ret + jnp.where(n < max_exact, n, large)


def _gate_weights(w_a, b_a, w_x, b_x):
    depth = w_a.shape[0]
    n_slabs = LRU_WIDTH // LANES
    per_slab = LANES // LRU_BLOCK_W

    def slab_blockdiag(w):
        w = w.reshape(depth, n_slabs, per_slab, LRU_BLOCK_W, LRU_BLOCK_W)
        eye = jnp.eye(per_slab, dtype=w.dtype)
        full = jnp.einsum('lspcd,pq->lspcqd', w, eye)
        return full.reshape(depth, n_slabs, LANES, LANES)

    mats = [slab_blockdiag(w[:, d]) for d in range(2) for w in (w_a, w_x)]
    wg = (0.5 * jnp.concatenate(mats, axis=-1)).astype(BF16)
    biases = [b[:, d].reshape(depth, n_slabs, 1, LANES) for d in range(2) for b in (b_a, b_x)]
    bg = 0.5 * jnp.concatenate(biases, axis=-1)
    bg_hi = bg.astype(BF16)
    bg_lo = (bg - bg_hi.astype(F32)).astype(BF16)
    unused = jnp.zeros((depth, n_slabs, LANES - GATE_BIAS_ROWS, 4 * LANES), BF16)
    return jnp.concatenate([wg, bg_hi, bg_lo, unused], axis=2)


def kernel(x, ffn1_norm, ffn1_w_gate, ffn1_w_up, ffn1_w_down, mix_norm, w_in, conv_w, conv_b, lru_w_a, lru_b_a, lru_w_x, lru_b_x, lru_lambda, attn_sink, rel_bias, lru_out_norm, attn_out_norm, w_out, ffn2_norm, ffn2_w_gate, ffn2_w_up, ffn2_w_down, final_norm):
    batch, seq, d_model = x.shape
    depth = ffn1_norm.shape[0]
    tokens = batch * seq
    assert d_model == D_MODEL and seq % BLOCK == 0 and tokens % FFN1_TILE == 0 and tokens % FFN2_TILE == 0
    xt = x.reshape(tokens, D_MODEL)
    bias = _rel_bias_table(_t5_bucket_index(), rel_bias)
    gate_w = _gate_weights(lru_w_a, lru_b_a, lru_w_x, lru_b_x)
    cb = conv_b.reshape(depth, 1, -1)
    nf = final_norm.reshape(1, -1)
    mix_out_norm = jnp.concatenate([lru_out_norm, attn_out_norm], axis=-1)
    g_gate1, g_ffn1, g_mix, g_gate2, g_ffn2, g_out = jnp.stack(
        [0.5 * ffn1_norm, ffn1_norm, mix_norm, 0.5 * ffn2_norm, ffn2_norm, mix_out_norm])[..., None]
    half_step = jnp.full((depth, D_FF, 1), FFN_RES, F32)
    ffn1_stacks = (ffn1_w_gate, ffn1_w_up, ffn1_w_down, w_in)
    ffn1_gains = (g_gate1, g_ffn1, half_step, g_mix)
    ffn2_stacks = (ffn2_w_gate, ffn2_w_up, ffn2_w_down, w_out)
    ffn2_gains = (g_gate2, g_ffn2, half_step, g_out)
    ffn1_w = tuple((w[0] * g[0]).astype(BF16) for w, g in zip(ffn1_stacks, ffn1_gains))
    for l in range(depth):
        (xt, xr, gate, q, k, v), ffn2_w = _ffn1_inproj(l, xt, *ffn1_w, ffn2_stacks, ffn2_gains)
        last = l == depth - 1
        y_rec, ffn1_w = _rec(l, xr, gate, conv_w, cb, gate_w, lru_lambda, batch, seq,
                             () if last else ffn1_stacks, () if last else ffn1_gains, l + 1)
        y_att = _attn(l, q, k, v, bias, attn_sink, batch, seq)
        xt = _outproj_ffn2(xt, y_rec, y_att, ffn2_w[3], *ffn2_w[:3], nf, final_norm=last)
    return xt.reshape(batch, seq, D_MODEL)
```

```python
import functools
import math

import jax
import jax.numpy as jnp
from jax import lax
from jax.experimental import pallas as pl
from jax.experimental.pallas import tpu as pltpu

F32 = jnp.float32
BF16 = jnp.bfloat16

D_MODEL = 1024
LRU_WIDTH = 512
LRU_BLOCK_W = 64
LRU_C = 8.0
CONV_WIDTH = 4
N_HEADS = 8
N_KV_HEADS = 2
KV_GROUP = N_HEADS // N_KV_HEADS
HEAD_DIM = 64
ATT_WIDTH = N_HEADS * HEAD_DIM
KV_WIDTH = N_KV_HEADS * HEAD_DIM
WINDOW = 128
BLOCK = 128
N_BUCKETS = 32
MAX_DISTANCE = 128
D_FF = 2816
FFN_RES = 0.5
EPS = 1e-6
NEG_INF = -1e30
D_IN = 2 * LRU_WIDTH + ATT_WIDTH + 2 * KV_WIDTH
LOG2E = math.log2(math.e)

LANES = 128
SUBLANES = 8
FF_CHUNK = 256
N_FF_CHUNKS = D_FF // FF_CHUNK
FFN1_TILE = 512
FFN2_TILE = 1024
VMEM_LIMIT = 56 * 1024 * 1024

SCAN_SEGS = 32
SCAN_VREGS = SCAN_SEGS // SUBLANES
SEG_PITCH = 129
REC_CHUNK = 512
GATE_BIAS_ROWS = 2


def _rms(x, g):
    return x * lax.rsqrt(jnp.mean(x * x, axis=-1, keepdims=True) + EPS) * g


def _silu_of_twice(half):
    return half + half * jnp.tanh(half)


def _inv_rms(x):
    return lax.rsqrt(jnp.mean(x * x, axis=-1, keepdims=True) + EPS)


def _swiglu_into(xn_ref, wg_ref, wu_ref, wd_ref, acc_ref, residual, unscaled=None):
    for c in range(N_FF_CHUNKS):
        cols = slice(c * FF_CHUNK, (c + 1) * FF_CHUNK)
        if c == 0 and unscaled is not None:
            raw_ref, inv_rms = unscaled
            half_g = inv_rms * jnp.dot(raw_ref[...], wg_ref[:, cols], preferred_element_type=F32)
            u = inv_rms * jnp.dot(raw_ref[...], wu_ref[:, cols], preferred_element_type=F32)
        else:
            half_g = jnp.dot(xn_ref[...], wg_ref[:, cols], preferred_element_type=F32)
            u = jnp.dot(xn_ref[...], wu_ref[:, cols], preferred_element_type=F32)
        act = (_silu_of_twice(half_g) * u).astype(BF16)
        part = jnp.dot(act, wd_ref[cols, :], preferred_element_type=F32)
        if c == 0:
            acc_ref[...] = residual + part
        else:
            acc_ref[...] += part


N_CAST = 4


def _cast_blocks(src_refs, dst_refs):
    n = len(dst_refs)
    for w, gain, dst in zip(src_refs[:n], src_refs[n:], dst_refs):
        dst[...] = (w[...] * gain[...]).astype(BF16)


def _ffn1_inproj_kernel(x_ref, wg_ref, wu_ref, wd_ref, win_ref, *rest):
    cast_src, rest = rest[:2 * N_CAST], rest[2 * N_CAST:]
    xo_ref, xr_ref, gate_ref, q_ref, k_ref, v_ref = rest[:6]
    cast_dst, (xb_ref, xraw_ref, acc_ref) = rest[6:6 + N_CAST], rest[6 + N_CAST:]
    _cast_blocks(cast_src, cast_dst)
    x = x_ref[...]
    xraw_ref[...] = x.astype(BF16)
    inv = _inv_rms(x)
    xb_ref[...] = (x * inv).astype(BF16)
    _swiglu_into(xb_ref, wg_ref, wu_ref, wd_ref, acc_ref, x, unscaled=(xraw_ref, inv))
    x1 = acc_ref[...]
    xo_ref[...] = x1
    xb_ref[...] = x1.astype(BF16)
    inv1 = _inv_rms(x1)

    def proj(rows, lo, width):
        return inv1[rows] * jnp.dot(xb_ref[rows, :], win_ref[:, lo:lo + width], preferred_element_type=F32)

    every = slice(None)
    tile = 2 * LANES
    for lo in range(0, LRU_WIDTH, tile):
        gate_ref[:, lo:lo + tile] = jax.nn.gelu(proj(every, LRU_WIDTH + lo, tile))
        xr_ref[:, lo:lo + tile] = proj(every, lo, tile)
    o = 2 * LRU_WIDTH
    q_ref[...] = (proj(every, o, ATT_WIDTH) * (HEAD_DIM ** -0.5 * LOG2E)).astype(BF16)
    o += ATT_WIDTH
    half = x_ref.shape[0] // 2
    for rows in (slice(0, half), slice(half, 2 * half)):
        kv = proj(rows, o, 2 * KV_WIDTH)
        k_ref[rows, :] = kv[:, :KV_WIDTH].astype(BF16)
        v_ref[rows, :] = kv[:, KV_WIDTH:].astype(BF16)


def _outproj_ffn2_kernel(x_ref, yr_ref, ya_ref, wo_ref, wg_ref, wu_ref, wd_ref, nf_ref,
                         xo_ref, xb_ref, acc_ref, *, final_norm):
    yr = yr_ref[...]
    y = jnp.concatenate([(yr * _inv_rms(yr)).astype(BF16), ya_ref[...]], axis=1)
    x1 = x_ref[...] + jnp.dot(y, wo_ref[...], preferred_element_type=F32)
    xb_ref[...] = (x1 * _inv_rms(x1)).astype(BF16)
    _swiglu_into(xb_ref, wg_ref, wu_ref, wd_ref, acc_ref, x1)
    x2 = acc_ref[...]
    if final_norm:
        x2 = _rms(x2, nf_ref[...])
    xo_ref[...] = x2


def _whole(arr):
    assert arr.ndim == 2
    return pl.BlockSpec(arr.shape, lambda *_: (0, 0), pipeline_mode=pl.Buffered(1))


def _rows(width, tm):
    return pl.BlockSpec((tm, width), lambda i: (i, 0))


def _cast_plan(stacks, gains, l, n_steps, step_of):
    bf16_rows = 2 * SUBLANES
    w_specs, g_specs, out_specs, out_shapes = [], [], [], []
    for w, gain in zip(stacks, gains):
        _, r, c = w.shape
        assert gain.shape == (w.shape[0], r, 1)
        n_blocks = n_steps
        while r % n_blocks or (r // n_blocks) % bf16_rows:
            n_blocks //= 2
        assert n_blocks >= 1 and n_steps % n_blocks == 0
        rep, rows = n_steps // n_blocks, r // n_blocks
        w_specs.append(pl.BlockSpec((None, rows, c), lambda *g, rep=rep: (l, step_of(*g) // rep, 0)))
        g_specs.append(pl.BlockSpec((None, rows, 1), lambda *g, rep=rep: (l, step_of(*g) // rep, 0)))
        out_specs.append(pl.BlockSpec((rows, c), lambda *g, rep=rep: (step_of(*g) // rep, 0)))
        out_shapes.append(jax.ShapeDtypeStruct((r, c), BF16))
    return w_specs + g_specs, out_specs, out_shapes


def _ffn1_inproj(l, x, wg, wu, wd, win, cast_stacks, cast_gains):
    t = x.shape[0]
    tm = FFN1_TILE
    out_shape = [
        jax.ShapeDtypeStruct((t, D_MODEL), F32),
        jax.ShapeDtypeStruct((t, LRU_WIDTH), F32),
        jax.ShapeDtypeStruct((t, LRU_WIDTH), F32),
        jax.ShapeDtypeStruct((t, ATT_WIDTH), BF16),
        jax.ShapeDtypeStruct((t, KV_WIDTH), BF16),
        jax.ShapeDtypeStruct((t, KV_WIDTH), BF16),
    ]
    out_specs = [_rows(D_MODEL, tm), _rows(LRU_WIDTH, tm), _rows(LRU_WIDTH, tm), _rows(ATT_WIDTH, tm),
                 _rows(KV_WIDTH, tm), _rows(KV_WIDTH, tm)]
    cast_in, cast_out, cast_shapes = _cast_plan(cast_stacks, cast_gains, l, t // tm, lambda i: i)
    outs = pl.pallas_call(
        _ffn1_inproj_kernel,
        grid=(t // tm,),
        in_specs=[_rows(D_MODEL, tm), _whole(wg), _whole(wu), _whole(wd), _whole(win)] + cast_in,
        out_specs=out_specs + cast_out,
        out_shape=out_shape + cast_shapes,
        scratch_shapes=[pltpu.VMEM((tm, D_MODEL), BF16), pltpu.VMEM((tm, D_MODEL), BF16),
                        pltpu.VMEM((tm, D_MODEL), F32)],
        compiler_params=pltpu.CompilerParams(dimension_semantics=("arbitrary",),
                                             vmem_limit_bytes=VMEM_LIMIT),
        name="ffn1_inproj",
    )(x, wg, wu, wd, win, *cast_stacks, *cast_gains)
    return outs[:6], outs[6:]


def _outproj_ffn2(x, yr, ya, wo, wg, wu, wd, nf, final_norm):
    t = x.shape[0]
    tm = FFN2_TILE
    return pl.pallas_call(
        functools.partial(_outproj_ffn2_kernel, final_norm=final_norm),
        grid=(t // tm,),
        in_specs=[_rows(D_MODEL, tm), _rows(LRU_WIDTH, tm), _rows(ATT_WIDTH, tm), _whole(wo),
                  _whole(wg), _whole(wu), _whole(wd), _whole(nf)],
        out_specs=_rows(D_MODEL, tm),
        out_shape=jax.ShapeDtypeStruct((t, D_MODEL), F32),
        scratch_shapes=[pltpu.VMEM((tm, D_MODEL), BF16), pltpu.VMEM((tm, D_MODEL), F32)],
        compiler_params=pltpu.CompilerParams(dimension_semantics=("arbitrary",),
                                             vmem_limit_bytes=VMEM_LIMIT),
        name="outproj_ffn2",
    )(x, yr, ya, wo, wg, wu, wd, nf)


def _rec_kernel(xr_ref, gate_ref, cw_ref, cb_ref, wg_ref, lam_ref, *rest, seq, n_cast):
    cast_src, o_ref = rest[:2 * n_cast], rest[2 * n_cast]
    cast_dst = rest[2 * n_cast + 1:3 * n_cast + 1]
    a_f, u_f, a_b, u_b, h_f, h_b = rest[3 * n_cast + 1:]
    _cast_blocks(cast_src, cast_dst)
    n_chunks = seq // REC_CHUNK
    n_steps = SEG_PITCH
    pad_rows = SCAN_SEGS * SEG_PITCH - seq
    chunk_row = lax.broadcasted_iota(jnp.int32, (REC_CHUNK, LANES), 0)

    def shifted_rows(c, offset):
        lo = c * REC_CHUNK + offset
        if lo >= 0 and lo + REC_CHUNK <= seq:
            return xr_ref[lo:lo + REC_CHUNK, :]
        rolled = pltpu.roll(xr_ref[c * REC_CHUNK:(c + 1) * REC_CHUNK, :], (-offset) % REC_CHUNK, 0)
        outside = (chunk_row < -offset) if offset < 0 else (chunk_row >= REC_CHUNK - offset)
        return jnp.where(outside, 0.0, rolled)

    for a_ref, u_ref in ((a_f, u_f), (a_b, u_b)):
        a_ref[seq:seq + pad_rows, :] = jnp.ones((pad_rows, LANES), F32)
        u_ref[seq:seq + pad_rows, :] = jnp.zeros((pad_rows, LANES), F32)

    lam = lam_ref[...]
    neg_softplus = -(jnp.maximum(-lam, 0.0) + jnp.log1p(jnp.exp(-jnp.abs(lam))))
    half_log2_a = (0.5 * LRU_C * LOG2E) * neg_softplus

    cw = cw_ref[...]
    cb = cb_ref[...]
    bias_cols = (lax.broadcasted_iota(jnp.int32, (REC_CHUNK, LANES), 1) < GATE_BIAS_ROWS).astype(BF16)
    for c in range(n_chunks):
        xc = cb
        for tap in range(CONV_WIDTH):
            xc = xc + cw[tap:tap + 1, :] * shifted_rows(c, tap - CONV_WIDTH // 2)
        lhs = jnp.concatenate([xc.astype(BF16), bias_cols], axis=1)
        th = jnp.tanh(jnp.dot(lhs, wg_ref[...], preferred_element_type=F32))
        half_xc = 0.5 * xc
        rows = slice(c * REC_CHUNK, (c + 1) * REC_CHUNK)
        for d, (a_ref, u_ref) in enumerate(((a_f, u_f), (a_b, u_b))):
            th_r = th[:, (2 * d) * LANES:(2 * d + 1) * LANES]
            th_i = th[:, (2 * d + 1) * LANES:(2 * d + 2) * LANES]
            scale = half_log2_a[d:d + 1, :]
            a = jnp.exp2(scale * th_r + scale)
            v = 1.0 - a * a
            root = jnp.where(v > 0.0, v * lax.rsqrt(v), 0.0)
            a_ref[rows, :] = a
            u_ref[rows, :] = root * (th_i * half_xc + half_xc)

    def seg_rows(j):
        return pl.ds(j, SCAN_SEGS, stride=SEG_PITCH)

    zeros = jnp.zeros((SCAN_SEGS, LANES), F32)
    ones = jnp.ones((SCAN_SEGS, LANES), F32)

    def segment_maps(j, carry):
        hf, pf, hb, pb = carry
        jb = n_steps - 1 - j
        af = a_f[seg_rows(j), :]
        ab = a_b[seg_rows(jb), :]
        hf = af * hf + u_f[seg_rows(j), :]
        hb = ab * hb + u_b[seg_rows(jb), :]
        return hf, pf * af, hb, pb * ab

    hf, pf, hb, pb = lax.fori_loop(0, n_steps, segment_maps, (zeros, ones, zeros, ones), unroll=4)

    seg = lax.broadcasted_iota(jnp.int32, (SCAN_SEGS, LANES), 0)

    def entering(h, p, toward_higher):
        d = 1
        while d < SCAN_SEGS:
            shift = d if toward_higher else SCAN_SEGS - d
            have = (seg >= d) if toward_higher else (seg < SCAN_SEGS - d)
            h = jnp.where(have, h + p * pltpu.roll(h, shift, 0), h)
            p = jnp.where(have, p * pltpu.roll(p, shift, 0), p)
            d *= 2
        one_step = 1 if toward_higher else SCAN_SEGS - 1
        edge = (seg == 0) if toward_higher else (seg == SCAN_SEGS - 1)
        return jnp.where(edge, 0.0, pltpu.roll(h, one_step, 0))

    hf0 = entering(hf, pf, True)
    hb0 = entering(hb, pb, False)

    def true_states(j, carry):
        hf, hb = carry
        jb = n_steps - 1 - j
        hf = a_f[seg_rows(j), :] * hf + u_f[seg_rows(j), :]
        hb = a_b[seg_rows(jb), :] * hb + u_b[seg_rows(jb), :]
        h_f[seg_rows(j), :] = hf
        h_b[seg_rows(jb), :] = hb
        return hf, hb

    lax.fori_loop(0, n_steps, true_states, (hf0, hb0), unroll=4)

    for c in range(n_chunks):
        rows = slice(c * REC_CHUNK, (c + 1) * REC_CHUNK)
        o_ref[rows, :] = gate_ref[rows, :] * (h_f[rows, :] + h_b[rows, :])


def _rec(l, xr, gate, cw, cb, wg, lam, batch, seq, cast_stacks, cast_gains, cast_layer):
    n_slabs = LRU_WIDTH // LANES
    assert SCAN_SEGS * SEG_PITCH >= seq > (SCAN_SEGS - 1) * SEG_PITCH and seq % REC_CHUNK == 0
    slab = pl.BlockSpec((seq, LANES), lambda b, s: (b, s))
    scan_rows = SCAN_SEGS * SEG_PITCH
    cast_in, cast_out, cast_shapes = _cast_plan(cast_stacks, cast_gains, cast_layer, batch * n_slabs,
                                                lambda b, s: b * n_slabs + s)
    outs = pl.pallas_call(
        functools.partial(_rec_kernel, seq=seq, n_cast=len(cast_stacks)),
        grid=(batch, n_slabs),
        in_specs=[slab, slab,
                  pl.BlockSpec((None, CONV_WIDTH, LANES), lambda b, s: (l, 0, s)),
                  pl.BlockSpec((None, 1, LANES), lambda b, s: (l, 0, s)),
                  pl.BlockSpec((None, None, 2 * LANES, 4 * LANES), lambda b, s: (l, s, 0, 0)),
                  pl.BlockSpec((None, 2, LANES), lambda b, s: (l, 0, s))] + cast_in,
        out_specs=[slab] + cast_out,
        out_shape=[jax.ShapeDtypeStruct((batch * seq, LRU_WIDTH), F32)] + cast_shapes,
        scratch_shapes=[pltpu.VMEM((scan_rows, LANES), F32)] * 6,
        compiler_params=pltpu.CompilerParams(dimension_semantics=("arbitrary", "arbitrary"),
                                             vmem_limit_bytes=VMEM_LIMIT),
        name="rg_lru",
    )(xr, gate, cw, cb, wg, lam, *cast_stacks, *cast_gains)
    return outs[0], outs[1:]


def _bias_kernel(buckets_ref, table_ref, o_ref):
    buckets = buckets_ref[...]
    for h in range(N_HEADS):
        acc = jnp.zeros(buckets.shape, F32)
        for b in range(N_BUCKETS):
            acc = jnp.where(buckets == b, table_ref[b, h] * LOG2E, acc)
        g, p, e = h // KV_GROUP, (h % KV_GROUP) // 2, h % 2
        o_ref[2 * g + e, p * BLOCK:(p + 1) * BLOCK, :] = acc


def _rel_bias_table(buckets, rel_bias):
    return pl.pallas_call(
        _bias_kernel,
        in_specs=[pl.BlockSpec(memory_space=pltpu.VMEM), pl.BlockSpec(memory_space=pltpu.SMEM)],
        out_specs=pl.BlockSpec(memory_space=pltpu.VMEM),
        out_shape=jax.ShapeDtypeStruct((N_HEADS // 2, 2 * BLOCK, 3 * BLOCK), F32),
        name="rel_bias",
    )(buckets, rel_bias)


ATT_SETUP_ROWS = 512
ATT_BLOCKS = 1
ATT_UNROLL = 4
ATT_PARTS = 2
HALF = LANES // 2


def _attn_kernel(q_ref, k_ref, v_ref, bias_ref, sink_ref, o_ref, kvar, vvar, *, seq, layer):
    nb = seq // BLOCK
    part = pl.program_id(1)

    @pl.when(part == 0)
    def _build_variants():
        lo_half = lax.broadcasted_iota(jnp.int32, (ATT_SETUP_ROWS, LANES), 1) < HALF
        zeros_blk = jnp.zeros((BLOCK, LANES), BF16)
        for u in range(2 * N_KV_HEADS):
            vvar[u, 0:BLOCK, :] = zeros_blk
            vvar[u, BLOCK + seq:2 * BLOCK + seq, :] = zeros_blk
            kvar[u, :, 0:BLOCK] = zeros_blk
            kvar[u, :, BLOCK + seq:2 * BLOCK + seq] = zeros_blk
        no_head = jnp.zeros((HEAD_DIM, ATT_SETUP_ROWS), BF16)
        for r in range(seq // ATT_SETUP_ROWS):
            src = slice(r * ATT_SETUP_ROWS, (r + 1) * ATT_SETUP_ROWS)
            dst = slice(BLOCK + r * ATT_SETUP_ROWS, BLOCK + (r + 1) * ATT_SETUP_ROWS)
            kt = k_ref[src, :].astype(F32).T.astype(BF16)
            vc = v_ref[src, :]
            vr = pltpu.roll(vc, HALF, 1)
            one = jnp.ones_like(vc)
            for g in range(N_KV_HEADS):
                head = kt[g * HEAD_DIM:(g + 1) * HEAD_DIM, :]
                kvar[2 * g, :, dst] = jnp.concatenate([head, no_head], axis=0)
                kvar[2 * g + 1, :, dst] = jnp.concatenate([no_head, head], axis=0)
            vvar[0, dst, :] = jnp.where(lo_half, vc, one)
            vvar[1, dst, :] = jnp.where(lo_half, one, vr)
            vvar[2, dst, :] = jnp.where(lo_half, vr, one)
            vvar[3, dst, :] = jnp.where(lo_half, one, vc)

    assert BLOCK == WINDOW == LANES
    t_idx = lax.broadcasted_iota(jnp.int32, (2 * BLOCK, LANES), 0) % BLOCK
    lane_minus_t = lax.broadcasted_iota(jnp.int32, (2 * BLOCK, LANES), 1) - t_idx
    first_head = lax.broadcasted_iota(jnp.int32, (2 * BLOCK, 1), 0) < BLOCK
    lo_out = lax.broadcasted_iota(jnp.int32, (BLOCK, LANES), 1) < HALF
    units = [(g, e) for g in range(N_KV_HEADS) for e in range(2)]
    never = 2 * LANES

    def blocks(i, carry):
        local = [i * ATT_BLOCKS + j for j in range(ATT_BLOCKS)]
        ns = [part * (nb // ATT_PARTS) + n for n in local]
        starts = [pl.multiple_of(n * BLOCK, BLOCK) for n in ns]
        rows_here = [pl.multiple_of(n * BLOCK, BLOCK) for n in local]
        work = [(j, g, e) for j in range(ATT_BLOCKS) for g, e in units]
        scores = []
        for j, g, e in work:
            qblk = q_ref[pl.ds(rows_here[j], BLOCK), (2 * g) * LANES:(2 * g + 2) * LANES]
            lhs = jnp.concatenate([qblk[:, :LANES], qblk[:, LANES:]], axis=0)
            kwin = kvar[2 * g + e, :, pl.ds(starts[j], 3 * BLOCK)]
            scores.append(jnp.dot(lhs, kwin, preferred_element_type=F32))
        probs, sink_mass = [], []
        for (j, g, e), s in zip(work, scores):
            u = 2 * g + e
            prev_min = jnp.where(ns[j] == 0, never, 0)
            next_max = jnp.where(ns[j] == nb - 1, -never, 0)
            s_prev = jnp.where(lane_minus_t >= prev_min, s[:, :BLOCK] + bias_ref[u, :, :BLOCK], NEG_INF)
            s_own = s[:, BLOCK:2 * BLOCK] + bias_ref[u, :, BLOCK:2 * BLOCK]
            s_next = jnp.where(lane_minus_t <= next_max, s[:, 2 * BLOCK:] + bias_ref[u, :, 2 * BLOCK:], NEG_INF)
            sink = jnp.where(first_head, sink_ref[layer, 4 * g + e], sink_ref[layer, 4 * g + 2 + e]) * LOG2E
            row_max = jnp.max(jnp.maximum(jnp.maximum(s_prev, s_own), s_next), axis=-1, keepdims=True)
            m = jnp.maximum(row_max, sink)
            p = jnp.concatenate([jnp.exp2(s_prev - m), jnp.exp2(s_own - m), jnp.exp2(s_next - m)], axis=-1)
            probs.append(p.astype(BF16))
            sink_mass.append(jnp.exp2(sink - m))
        outs = []
        for (j, g, e), p in zip(work, probs):
            vwin = vvar[2 * g + e, pl.ds(starts[j], 3 * BLOCK), :]
            outs.append(jnp.dot(p, vwin, preferred_element_type=F32))
        for j in range(ATT_BLOCKS):
            chunks = []
            for g in range(N_KV_HEADS):
                for p in range(2):
                    rows = slice(p * BLOCK, (p + 1) * BLOCK)
                    ue, uo = j * len(units) + 2 * g, j * len(units) + 2 * g + 1
                    even, odd = outs[ue][rows], outs[uo][rows]
                    num = jnp.where(lo_out, even, odd)
                    den = (pltpu.roll(jnp.where(lo_out, odd, even), HALF, 1)
                           + jnp.where(lo_out, sink_mass[ue][rows], sink_mass[uo][rows]))
                    chunks.append(num / den)
            y = jnp.concatenate(chunks, axis=-1)
            o_ref[pl.ds(rows_here[j], BLOCK), :] = (y * _inv_rms(y)).astype(BF16)
        return carry

    lax.fori_loop(0, nb // (ATT_PARTS * ATT_BLOCKS), blocks, 0, unroll=ATT_UNROLL)


def _attn(l, q, k, v, bias, sink, batch, seq):
    assert seq % ATT_SETUP_ROWS == 0 and (seq // BLOCK) % (ATT_PARTS * ATT_BLOCKS * ATT_UNROLL) == 0

    def seq_rows(width):
        return pl.BlockSpec((seq, width), lambda b, part: (b, 0))

    def part_rows(width):
        return pl.BlockSpec((seq // ATT_PARTS, width), lambda b, part: (b * ATT_PARTS + part, 0))

    return pl.pallas_call(
        functools.partial(_attn_kernel, seq=seq, layer=l),
        grid=(batch, ATT_PARTS),
        in_specs=[part_rows(ATT_WIDTH), seq_rows(KV_WIDTH), seq_rows(KV_WIDTH),
                  pl.BlockSpec(bias.shape, lambda b, part: (0, 0, 0), pipeline_mode=pl.Buffered(1)),
                  pl.BlockSpec(memory_space=pltpu.SMEM)],
        out_specs=part_rows(ATT_WIDTH),
        out_shape=jax.ShapeDtypeStruct((batch * seq, ATT_WIDTH), BF16),
        scratch_shapes=[pltpu.VMEM((2 * N_KV_HEADS, LANES, seq + 2 * BLOCK), BF16),
                        pltpu.VMEM((2 * N_KV_HEADS, seq + 2 * BLOCK, LANES), BF16)],
        compiler_params=pltpu.CompilerParams(dimension_semantics=("arbitrary", "arbitrary"),
                                             vmem_limit_bytes=VMEM_LIMIT),
        name="windowed_gqa",
    )(q, k, v, bias, sink)


def _t5_bucket_index():
    t = jnp.arange(BLOCK)[:, None]
    j = jnp.arange(3 * BLOCK)[None, :]
    rel = j - BLOCK - t
    half = N_BUCKETS // 2
    max_exact = half // 2
    ret = (rel > 0).astype(jnp.int32) * half
    n = jnp.abs(rel)
    n_f = jnp.maximum(n, 1).astype(jnp.float32)
    large = max_exact + (jnp.log(n_f / max_exact) / math.log(MAX_DISTANCE / max_exact)
                         * (half - max_exact)).astype(jnp.int32)
    large = jnp.minimum(large, half - 1)
    return ret + jnp.where(n < max_exact, n, large)


def _gate_weights(w_a, b_a, w_x, b_x):
    depth = w_a.shape[0]
    n_slabs = LRU_WIDTH // LANES
    per_slab = LANES // LRU_BLOCK_W

    def slab_blockdiag(w):
        w = w.reshape(depth, n_slabs, per_slab, LRU_BLOCK_W, LRU_BLOCK_W)
        eye = jnp.eye(per_slab, dtype=w.dtype)
        full = jnp.einsum('lspcd,pq->lspcqd', w, eye)
        return full.reshape(depth, n_slabs, LANES, LANES)

    mats = [slab_blockdiag(w[:, d]) for d in range(2) for w in (w_a, w_x)]
    wg = (0.5 * jnp.concatenate(mats, axis=-1)).astype(BF16)
    biases = [b[:, d].reshape(depth, n_slabs, 1, LANES) for d in range(2) for b in (b_a, b_x)]
    bg = 0.5 * jnp.concatenate(biases, axis=-1)
    bg_hi = bg.astype(BF16)
    bg_lo = (bg - bg_hi.astype(F32)).astype(BF16)
    unused = jnp.zeros((depth, n_slabs, LANES - GATE_BIAS_ROWS, 4 * LANES), BF16)
    return jnp.concatenate([wg, bg_hi, bg_lo, unused], axis=2)


def kernel(x, ffn1_norm, ffn1_w_gate, ffn1_w_up, ffn1_w_down, mix_norm, w_in, conv_w, conv_b, lru_w_a, lru_b_a, lru_w_x, lru_b_x, lru_lambda, attn_sink, rel_bias, lru_out_norm, attn_out_norm, w_out, ffn2_norm, ffn2_w_gate, ffn2_w_up, ffn2_w_down, final_norm):
    batch, seq, d_model = x.shape
    depth = ffn1_norm.shape[0]
    tokens = batch * seq
    assert d_model == D_MODEL and seq % BLOCK == 0 and tokens % FFN1_TILE == 0 and tokens % FFN2_TILE == 0
    xt = x.reshape(tokens, D_MODEL)
    bias = _rel_bias_table(_t5_bucket_index(), rel_bias)
    gate_w = _gate_weights(lru_w_a, lru_b_a, lru_w_x, lru_b_x)
    cb = conv_b.reshape(depth, 1, -1)
    nf = final_norm.reshape(1, -1)
    mix_out_norm = jnp.concatenate([lru_out_norm, attn_out_norm], axis=-1)
    g_gate1, g_ffn1, g_mix, g_gate2, g_ffn2, g_out = jnp.stack(
        [0.5 * ffn1_norm, ffn1_norm, mix_norm, 0.5 * ffn2_norm, ffn2_norm, mix_out_norm])[..., None]
    half_step = jnp.full((depth, D_FF, 1), FFN_RES, F32)
    ffn1_stacks = (ffn1_w_gate, ffn1_w_up, ffn1_w_down, w_in)
    ffn1_gains = (g_gate1, g_ffn1, half_step, g_mix)
    ffn2_stacks = (ffn2_w_gate, ffn2_w_up, ffn2_w_down, w_out)
    ffn2_gains = (g_gate2, g_ffn2, half_step, g_out)
    ffn1_w = tuple((w[0] * g[0]).astype(BF16) for w, g in zip(ffn1_stacks, ffn1_gains))
    for l in range(depth):
        (xt, xr, gate, q, k, v), ffn2_w = _ffn1_inproj(l, xt, *ffn1_w, ffn2_stacks, ffn2_gains)
        last = l == depth - 1
        y_rec, ffn1_w = _rec(l, xr, gate, conv_w, cb, gate_w, lru_lambda, batch, seq,
                             () if last else ffn1_stacks, () if last else ffn1_gains, l + 1)
        y_att = _attn(l, q, k, v, bias, attn_sink, batch, seq)
        xt = _outproj_ffn2(xt, y_rec, y_att, ffn2_w[3], *ffn2_w[:3], nf, final_norm=last)
    return xt.reshape(batch, seq, D_MODEL)
```
